```python
import math
import jax, jax.numpy as jnp
from jax import lax
import numpy as np

D_MODEL = 2048
BATCH = 4
SEQ = 4096
DEPTH = 4

GRID_W = 64
CTX_LEN = 256
HEAD_DIM = 128
W_GDN = D_MODEL // 4
W_GQA = D_MODEL // 2
W_LRU = D_MODEL - W_GDN - W_GQA
GDN_HEADS = W_GDN // HEAD_DIM
GDN_CHUNK = 64
GDN_CONV = 4
GQA_Q_HEADS = W_GQA // HEAD_DIM
GQA_KV_HEADS = GQA_Q_HEADS // 4
Q_BLOCK = 128
ROPE_THETA = 10000.0
LRU_BLOCKS = 8
LRU_BLOCK = W_LRU // LRU_BLOCKS
LRU_CONV = 4
LRU_C = 8.0
FFN_DIM = ((8 * D_MODEL // 3 + 127) // 128) * 128
FFN_CONV = 3
N_MOD = 6
EPS = 1e-6
IN_SPLITS = (3 * W_GDN, W_GDN, GDN_HEADS, GDN_HEADS, GDN_HEADS, GDN_HEADS,
             W_GQA, GQA_KV_HEADS * HEAD_DIM, GQA_KV_HEADS * HEAD_DIM, W_LRU, W_LRU)
IN_COLS = 4 * W_GDN + 4 * GDN_HEADS + W_GQA + 2 * GQA_KV_HEADS * HEAD_DIM + 2 * W_LRU

kernel_name = 'hybrid_parallel_heads_flow_block'


def rmsnorm(x, g):
    xf = x.astype(jnp.float32)
    y = xf * lax.rsqrt(jnp.mean(xf * xf, axis=-1, keepdims=True) + EPS)
    return y.astype(x.dtype) * g


def l2norm(x):
    xf = x.astype(jnp.float32)
    return (xf * lax.rsqrt(jnp.sum(xf * xf, axis=-1, keepdims=True) + EPS)).astype(x.dtype)


def modulate(h, shift, scale):
    return h * (1 + scale) + shift


def adaln(cond, w, b):
    return jax.nn.silu(cond) @ w + b


def depthwise_conv(x, w, pad_left):
    width, t = w.shape[0], x.shape[1]
    xp = jnp.pad(x, ((0, 0), (pad_left, width - 1 - pad_left), (0, 0)))
    y = xp[:, 0:t] * w[0]
    for j in range(1, width):
        y = y + xp[:, j:j + t] * w[j]
    return y


def identity(t):
    return t


def reverse(t):
    return jnp.flip(t, axis=1)


def split_columns(p):
    bounds, acc = [], 0
    for s in IN_SPLITS[:-1]:
        acc += s
        bounds.append(acc)
    return jnp.split(p, bounds, axis=-1)


def axial_rope_tables(n_tokens):
    rows = n_tokens // GRID_W
    row = jnp.repeat(jnp.arange(rows, dtype=jnp.float32), GRID_W)
    col = jnp.tile(jnp.arange(GRID_W, dtype=jnp.float32), rows)
    axis_dim = HEAD_DIM // 2
    inv_freq = ROPE_THETA ** (-jnp.arange(0, axis_dim, 2, dtype=jnp.float32) / axis_dim)
    ang = jnp.concatenate([row[:, None] * inv_freq, col[:, None] * inv_freq], axis=-1)
    return jnp.cos(ang), jnp.sin(ang)


def apply_axial_rope(x, cos, sin):
    b, t, h, d = x.shape
    xr = x.reshape(b, t, h, 2, 2, d // 4)
    x1, x2 = xr[..., 0, :], xr[..., 1, :]
    c = cos.reshape(t, 1, 2, d // 4)
    s = sin.reshape(t, 1, 2, d // 4)
    out = jnp.stack([x1 * c - x2 * s, x2 * c + x1 * s], axis=-2)
    return out.reshape(b, t, h, d).astype(x.dtype)


def gated_delta_chunked(q, k, v, log_a, beta, s0):
    b, t, h, dk = q.shape
    dv = v.shape[-1]
    n = t // GDN_CHUNK

    def chunks(z):
        z = z.astype(jnp.float32).reshape(b, n, GDN_CHUNK, h, *z.shape[3:])
        return jnp.moveaxis(z, (1, 3), (0, 2))

    qc, kc, vc = chunks(q), chunks(k), chunks(v)
    la, bt = chunks(log_a), chunks(beta)
    g = jnp.cumsum(la, axis=-1)
    idx = jnp.arange(GDN_CHUNK)
    incl = idx[:, None] >= idx[None, :]
    strict = idx[:, None] > idx[None, :]
    dmask = jnp.exp(jnp.where(incl, g[..., :, None] - g[..., None, :], -jnp.inf))
    kb = kc * bt[..., None]
    lower = jnp.einsum('nbhcd,nbhsd->nbhcs', kb, kc) * jnp.where(strict, dmask, 0.0)
    lhs = lower + jnp.eye(GDN_CHUNK, dtype=jnp.float32)
    rhs = jnp.concatenate([vc * bt[..., None], kb * jnp.exp(g)[..., None]], axis=-1)
    sol = lax.linalg.triangular_solve(lhs, rhs, left_side=True, lower=True, unit_diagonal=True)
    u, w = sol[..., :dv], sol[..., dv:]
    attn = jnp.einsum('nbhcd,nbhsd->nbhcs', qc, kc) * dmask
    q_dec = qc * jnp.exp(g)[..., None]
    g_last = g[..., -1]
    k_dec = kc * jnp.exp(g_last[..., None] - g)[..., None]

    def step(s, xs):
        u_i, w_i, q_i, a_i, k_i, gl_i = xs
        v_new = u_i - jnp.einsum('bhcd,bhde->bhce', w_i, s)
        o_i = jnp.einsum('bhcd,bhde->bhce', q_i, s) + jnp.einsum('bhcs,bhse->bhce', a_i, v_new)
        s = s * jnp.exp(gl_i)[..., None, None] + jnp.einsum('bhcd,bhce->bhde', k_i, v_new)
        return s, o_i

    s_final, o = lax.scan(step, s0.astype(jnp.float32), (u, w, q_dec, attn, k_dec, g_last))
    o = jnp.moveaxis(o, (0, 2), (1, 3)).reshape(b, t, h, dv)
    return o.astype(v.dtype), s_final


def gdn_stream(parts, conv_w, a_log, dt_bias):
    qkv, z, b_f, b_b, a_f, a_b = parts
    b, t, _ = qkv.shape
    qkv = jax.nn.silu(depthwise_conv(qkv, conv_w, GDN_CONV // 2))
    q, k, v = jnp.split(qkv, 3, axis=-1)
    q = l2norm(q.reshape(b, t, GDN_HEADS, HEAD_DIM)) * (HEAD_DIM ** -0.5)
    k = l2norm(k.reshape(b, t, GDN_HEADS, HEAD_DIM))
    v = v.reshape(b, t, GDN_HEADS, HEAD_DIM)
    a_log = a_log.astype(jnp.float32)
    dt_bias = dt_bias.astype(jnp.float32)
    dirs = []
    for d, (a_raw, b_raw) in enumerate(((a_f, b_f), (a_b, b_b))):
        log_a = -jnp.exp(a_log[d]) * jax.nn.softplus(a_raw.astype(jnp.float32) + dt_bias[d])
        dirs.append((log_a, jax.nn.sigmoid(b_raw.astype(jnp.float32))))
    return q, k, v, z, dirs


def gated_rmsnorm(o, z, g):
    b, t, h, dv = o.shape
    y = rmsnorm(o, g) * jax.nn.silu(z.reshape(b, t, h, dv))
    return y.reshape(b, t, h * dv)


def gdn_mixer(parts_l, parts_c, conv_w, a_log, dt_bias, norm_g, ctx_out):
    ql, kl, vl, zl, dl = gdn_stream(parts_l, conv_w, a_log, dt_bias)
    qc, kc, vc, zc, dc = gdn_stream(parts_c, conv_w, a_log, dt_bias)
    s0 = jnp.zeros((qc.shape[0], GDN_HEADS, HEAD_DIM, HEAD_DIM), jnp.float32)
    outs_l, outs_c = [], []
    for d, f in enumerate((identity, reverse)):
        (la_c, bt_c), (la_l, bt_l) = dc[d], dl[d]
        oc, sc = gated_delta_chunked(f(qc), f(kc), f(vc), f(la_c), f(bt_c), s0)
        ol, _ = gated_delta_chunked(f(ql), f(kl), f(vl), f(la_l), f(bt_l), sc)
        outs_l.append(f(ol))
        outs_c.append(f(oc))
    y_l = gated_rmsnorm(outs_l[0] + outs_l[1], zl, norm_g)
    y_c = gated_rmsnorm(outs_c[0] + outs_c[1], zc, norm_g) if ctx_out else None
    return y_l, y_c


def block_attention(q, k, v):
    b, t, hq, d = q.shape
    hkv = k.shape[2]
    grp = hq // hkv
    nb = t // Q_BLOCK
    qb = q.reshape(b, nb, Q_BLOCK, hkv, grp, d).swapaxes(0, 1)
    scale = d ** -0.5

    def one_block(qi):
        s = jnp.einsum('bqkgd,bskd->bkgqs', qi, k).astype(jnp.float32) * scale
        p = jax.nn.softmax(s, axis=-1).astype(v.dtype)
        return jnp.einsum('bkgqs,bskd->bqkgd', p, v)

    o = lax.map(one_block, qb)
    return o.swapaxes(0, 1).reshape(b, t, hq * d)


def gqa_mixer(parts_l, parts_c, q_norm_g, k_norm_g, cos, sin, ctx_out):
    def heads(q, k, v):
        b, t, _ = q.shape
        return (rmsnorm(q.reshape(b, t, GQA_Q_HEADS, HEAD_DIM), q_norm_g),
                rmsnorm(k.reshape(b, t, GQA_KV_HEADS, HEAD_DIM), k_norm_g),
                v.reshape(b, t, GQA_KV_HEADS, HEAD_DIM))

    ql, kl, vl = heads(*parts_l)
    qc, kc, vc = heads(*parts_c)
    ql = apply_axial_rope(ql, cos, sin)
    kl = apply_axial_rope(kl, cos, sin)
    y_l = block_attention(ql, jnp.concatenate([kl, kc], axis=1), jnp.concatenate([vl, vc], axis=1))
    y_c = block_attention(qc, kc, vc) if ctx_out else None
    return y_l, y_c


def linear_combine(left, right):
    a1, b1 = left
    a2, b2 = right
    return a1 * a2, a2 * b1 + b2


def rglru(x, gate_w, gate_b, lam, h0):
    b, t, _ = x.shape
    xb = x.reshape(b, t, LRU_BLOCKS, LRU_BLOCK)
    gates = jnp.einsum('btnd,gnde->gbtne', xb, gate_w).reshape(2, b, t, W_LRU) + gate_b[:, None, None, :]
    gates = jax.nn.sigmoid(gates.astype(jnp.float32))
    r, i = gates[0], gates[1]
    log_a = -LRU_C * r * jax.nn.softplus(-lam.astype(jnp.float32))
    a = jnp.exp(log_a)
    u = jnp.sqrt(-jnp.expm1(2.0 * log_a)) * (i * x.astype(jnp.float32))
    u = u.at[:, 0].add(a[:, 0] * h0)
    _, h = lax.associative_scan(linear_combine, (a, u), axis=1)
    return h.astype(x.dtype), h[:, -1]


def lru_mixer(parts_l, parts_c, conv_w, conv_b, gate_w, gate_b, lam, ctx_out):
    xl, gl = parts_l
    xc, gc = parts_c
    xl = depthwise_conv(xl, conv_w, LRU_CONV // 2) + conv_b
    xc = depthwise_conv(xc, conv_w, LRU_CONV // 2) + conv_b
    h0 = jnp.zeros((xc.shape[0], W_LRU), jnp.float32)
    outs_l, outs_c = [], []
    for d, f in enumerate((identity, reverse)):
        hc, hc_last = rglru(f(xc), gate_w[d], gate_b[d], lam[d], h0)
        hl, _ = rglru(f(xl), gate_w[d], gate_b[d], lam[d], hc_last)
        outs_l.append(f(hl))
        outs_c.append(f(hc))
    y_l = jax.nn.gelu(gl) * (outs_l[0] + outs_l[1])
    y_c = jax.nn.gelu(gc) * (outs_c[0] + outs_c[1]) if ctx_out else None
    return y_l, y_c


def conv_ffn(h, w_up, conv_w, conv_b, w_down):
    u = depthwise_conv(h @ w_up, conv_w, FFN_CONV // 2) + conv_b
    gate, up = jnp.split(u, 2, axis=-1)
    return (jax.nn.silu(gate) * up) @ w_down


def trunk_layer(x, ctx, mod_l, mod_c, cos, sin, norm1_g, norm2_g, w_in,
                gdn_conv_w, gdn_a_log, gdn_dt_bias, gdn_norm_g, q_norm_g, k_norm_g,
                lru_conv_w, lru_conv_b, lru_gate_w, lru_gate_b, lru_lambda,
                w_out, ffn_w_up, ffn_conv_w, ffn_conv_b, ffn_w_down, ctx_out):
    sh1_l, sc1_l, g1_l, sh2_l, sc2_l, g2_l = jnp.split(mod_l, N_MOD, axis=-1)
    sh1_c, sc1_c, g1_c, sh2_c, sc2_c, g2_c = jnp.split(mod_c, N_MOD, axis=-1)
    h_l = modulate(rmsnorm(x, norm1_g), sh1_l, sc1_l)
    h_c = modulate(rmsnorm(ctx, norm1_g), sh1_c, sc1_c)
    p_l = split_columns(h_l @ w_in)
    p_c = split_columns(h_c @ w_in)
    ya_l, ya_c = gdn_mixer(p_l[0:6], p_c[0:6], gdn_conv_w, gdn_a_log, gdn_dt_bias, gdn_norm_g, ctx_out)
    yb_l, yb_c = gqa_mixer(p_l[6:9], p_c[6:9], q_norm_g, k_norm_g, cos, sin, ctx_out)
    yc_l, yc_c = lru_mixer(p_l[9:11], p_c[9:11], lru_conv_w, lru_conv_b, lru_gate_w, lru_gate_b, lru_lambda, ctx_out)
    x = x + g1_l * (jnp.concatenate([ya_l, yb_l, yc_l], axis=-1) @ w_out)
    x = x + g2_l * conv_ffn(modulate(rmsnorm(x, norm2_g), sh2_l, sc2_l), ffn_w_up, ffn_conv_w, ffn_conv_b, ffn_w_down)
    if ctx_out:
        ctx = ctx + g1_c * (jnp.concatenate([ya_c, yb_c, yc_c], axis=-1) @ w_out)
        ctx = ctx + g2_c * conv_ffn(modulate(rmsnorm(ctx, norm2_g), sh2_c, sc2_c), ffn_w_up, ffn_conv_w, ffn_conv_b, ffn_w_down)
    return x, ctx


def setup_inputs(seed: int = 0) -> dict:
    key = jax.random.key(seed)
    ks = jax.random.split(key, 32)
    f32 = jnp.float32

    def nrm(k, shape, scale):
        return jax.random.normal(k, shape, f32) * scale

    def gain(k, shape):
        return 1.0 + 0.02 * jax.random.normal(k, shape, f32)

    a_init = jax.random.uniform(ks[8], (DEPTH, 2, GDN_HEADS), f32, 1.0, 16.0)
    dt = jnp.exp(jax.random.uniform(ks[9], (DEPTH, 2, GDN_HEADS), f32, math.log(1e-3), math.log(1e-1)))
    a_lru = jax.random.uniform(ks[17], (DEPTH, 2, W_LRU), f32, 0.9, 0.999) ** (1.0 / LRU_C)
    return {
        'x': nrm(ks[0], (BATCH, SEQ, D_MODEL), 1.0),
        'c': nrm(ks[1], (BATCH, D_MODEL), 1.0),
        'ctx': nrm(ks[2], (BATCH, CTX_LEN, D_MODEL), 1.0),
        'c_ctx': nrm(ks[3], (D_MODEL,), 1.0),
        'ada_w': nrm(ks[4], (DEPTH, D_MODEL, N_MOD * D_MODEL), 0.5 * D_MODEL ** -0.5),
        'ada_b': nrm(ks[5], (DEPTH, N_MOD * D_MODEL), 0.02),
        'norm1_g': gain(ks[6], (DEPTH, D_MODEL)),
        'norm2_g': gain(ks[7], (DEPTH, D_MODEL)),
        'w_in': nrm(ks[10], (DEPTH, D_MODEL, IN_COLS), D_MODEL ** -0.5),
        'gdn_conv_w': nrm(ks[11], (DEPTH, GDN_CONV, 3 * W_GDN), GDN_CONV ** -0.5),
        'gdn_a_log': jnp.log(a_init),
        'gdn_dt_bias': dt + jnp.log(-jnp.expm1(-dt)),
        'gdn_norm_g': gain(ks[12], (DEPTH, HEAD_DIM)),
        'q_norm_g': gain(ks[13], (DEPTH, HEAD_DIM)),
        'k_norm_g': gain(ks[14], (DEPTH, HEAD_DIM)),
        'lru_conv_w': nrm(ks[15], (DEPTH, LRU_CONV, W_LRU), LRU_CONV ** -0.5),
        'lru_conv_b': nrm(ks[16], (DEPTH, W_LRU), 0.02),
        'lru_gate_w': nrm(ks[18], (DEPTH, 2, 2, LRU_BLOCKS, LRU_BLOCK, LRU_BLOCK), LRU_BLOCK ** -0.5),
        'lru_gate_b': nrm(ks[19], (DEPTH, 2, 2, W_LRU), 0.02),
        'lru_lambda': jnp.log(a_lru) - jnp.log1p(-a_lru),
        'w_out': nrm(ks[20], (DEPTH, D_MODEL, D_MODEL), D_MODEL ** -0.5),
        'ffn_w_up': nrm(ks[21], (DEPTH, D_MODEL, 2 * FFN_DIM), D_MODEL ** -0.5),
        'ffn_conv_w': nrm(ks[22], (DEPTH, FFN_CONV, 2 * FFN_DIM), FFN_CONV ** -0.5),
        'ffn_conv_b': nrm(ks[23], (DEPTH, 2 * FFN_DIM), 0.02),
        'ffn_w_down': nrm(ks[24], (DEPTH, FFN_DIM, D_MODEL), FFN_DIM ** -0.5),
        'final_norm_g': gain(ks[25], (D_MODEL,)),
    }


def reference(x, c, ctx, c_ctx, ada_w, ada_b, norm1_g, norm2_g, w_in, gdn_conv_w, gdn_a_log, gdn_dt_bias,
              gdn_norm_g, q_norm_g, k_norm_g, lru_conv_w, lru_conv_b, lru_gate_w, lru_gate_b, lru_lambda,
              w_out, ffn_w_up, ffn_conv_w, ffn_conv_b, ffn_w_down, final_norm_g):
    cos, sin = axial_rope_tables(x.shape[1])
    for layer in range(DEPTH):
        mod_l = adaln(c, ada_w[layer], ada_b[layer])[:, None, :]
        mod_c = adaln(c_ctx, ada_w[layer], ada_b[layer])[None, None, :]
        x, ctx = trunk_layer(
            x, ctx, mod_l, mod_c, cos, sin, norm1_g[layer], norm2_g[layer], w_in[layer],
            gdn_conv_w[layer], gdn_a_log[layer], gdn_dt_bias[layer], gdn_norm_g[layer],
            q_norm_g[layer], k_norm_g[layer], lru_conv_w[layer], lru_conv_b[layer],
            lru_gate_w[layer], lru_gate_b[layer], lru_lambda[layer], w_out[layer],
            ffn_w_up[layer], ffn_conv_w[layer], ffn_conv_b[layer], ffn_w_down[layer],
            layer < DEPTH - 1)
    return rmsnorm(x, final_norm_g)
```

```python
import functools
import math
from typing import NamedTuple

import jax
import jax.numpy as jnp
import numpy as np
from jax import lax
from jax.experimental import pallas as pl
from jax.experimental.pallas import tpu as pltpu

F32 = jnp.float32
BF16 = jnp.bfloat16
EPS = 1e-6
HD = 128
LANES = 128
N_MOD = 6
GDN_CONV = 4
LRU_CONV = 4
LRU_C = 8.0
LRU_BLOCKS = 8
ROPE_THETA = 10000.0
CH = 64
CONV_PAD = 8
VMEM_LIMIT = 56 * 1024 * 1024


class Dims(NamedTuple):
    d: int
    batch: int
    seq: int
    ctx: int
    depth: int
    grid_w: int

    @property
    def w_gdn(self):
        return self.d // 4

    @property
    def w_gqa(self):
        return self.d // 2

    @property
    def w_lru(self):
        return self.d - self.w_gdn - self.w_gqa

    @property
    def gdn_heads(self):
        return self.w_gdn // HD

    @property
    def q_heads(self):
        return self.w_gqa // HD

    @property
    def kv_heads(self):
        return self.q_heads // 4

    @property
    def ffn(self):
        return ((8 * self.d // 3 + 127) // 128) * 128

    @property
    def ffn_pad(self):
        return ((self.ffn + 511) // 512) * 512

    @property
    def p_cols(self):
        return 4 * self.w_gdn + self.w_gqa + 2 * self.kv_heads * HD + 2 * self.w_lru

    @property
    def off_z(self):
        return 3 * self.w_gdn

    @property
    def off_q(self):
        return 4 * self.w_gdn

    @property
    def off_k(self):
        return self.off_q + self.w_gqa

    @property
    def off_v(self):
        return self.off_k + self.kv_heads * HD

    @property
    def off_lx(self):
        return self.off_v + self.kv_heads * HD

    @property
    def off_lg(self):
        return self.off_lx + self.w_lru


def _cparams(sem, vmem=VMEM_LIMIT):
    return pltpu.CompilerParams(dimension_semantics=sem, vmem_limit_bytes=vmem)


def _dot(a, b):
    return jnp.dot(a, b, preferred_element_type=F32)


def _dot_exact(a, b):
    return jnp.dot(a, b, preferred_element_type=F32, precision=lax.Precision.HIGHEST)


def _dot_nt(a, b):
    return lax.dot_general(a, b, (((1,), (1,)), ((), ())), preferred_element_type=F32)


def _dot_tn(a, b):
    return lax.dot_general(a, b, (((0,), (0,)), ((), ())), preferred_element_type=F32)


def _silu(x):
    return x * jax.nn.sigmoid(x)


def _rms_mod(x, g, sh, sc):
    ms = jnp.mean(x * x, axis=-1, keepdims=True)
    return (x * lax.rsqrt(ms + EPS)) * g * (1.0 + sc) + sh


def _adaln_kernel(c_ref, w_ref, b_ref, o_ref):
    s = _silu(c_ref[...]).astype(BF16)
    o_ref[...] = _dot(s, w_ref[...].astype(BF16)) + b_ref[...]


def _adaln(cond8, ada_w, ada_b, tn=1024):
    depth, d, n = ada_w.shape
    tn = min(tn, n)
    return pl.pallas_call(
        _adaln_kernel,
        grid=(depth, n // tn),
        in_specs=[
            pl.BlockSpec((8, d), lambda l, j: (0, 0)),
            pl.BlockSpec((None, d, tn), lambda l, j: (l, 0, j)),
            pl.BlockSpec((None, 1, tn), lambda l, j: (l, 0, j)),
        ],
        out_specs=pl.BlockSpec((None, 8, tn), lambda l, j: (l, 0, j)),
        out_shape=jax.ShapeDtypeStruct((depth, 8, n), F32),
        compiler_params=_cparams(("parallel", "parallel")),
        name="adaln",
    )(cond8, ada_w, ada_b.reshape(depth, 1, n))


def _mod_spec(layer, k, mrow, d, tn=None):
    if tn is None:
        return pl.BlockSpec((None, None, None, 1, d), lambda i, j: (layer, mrow(i), k, 0, 0))
    return pl.BlockSpec((None, None, None, 1, tn), lambda i, j: (layer, mrow(i), k, 0, j))


def _in_proj_kernel(x_ref, g_ref, sh_ref, sc_ref, w_ref, wg_ref, p_ref, gate_ref, h_scr, *, rc):
    tm = x_ref.shape[0]

    @pl.when(pl.program_id(1) == 0)
    def _():
        def body(r, carry):
            rows = pl.ds(pl.multiple_of(r * rc, rc), rc)
            h = _rms_mod(x_ref[rows, :], g_ref[...], sh_ref[...], sc_ref[...]).astype(BF16)
            h_scr[rows, :] = h
            gate_ref[rows, :] = _dot(h, wg_ref[...])
            return carry

        lax.fori_loop(0, tm // rc, body, 0)

    p_ref[...] = _dot(h_scr[...], w_ref[...]).astype(p_ref.dtype)


def _in_proj(x, mods, layer, mrow, norm_g, w_main, w_gate, tm, tn=512):
    rows, d = x.shape
    n = w_main.shape[1]
    tm = min(tm, rows)
    assert n % tn == 0 and rows % tm == 0
    return pl.pallas_call(
        functools.partial(_in_proj_kernel, rc=min(256, tm)),
        grid=(rows // tm, n // tn),
        in_specs=[
            pl.BlockSpec((tm, d), lambda i, j: (i, 0)),
            pl.BlockSpec((1, d), lambda i, j: (0, 0)),
            _mod_spec(layer, 0, mrow, d),
            _mod_spec(layer, 1, mrow, d),
            pl.BlockSpec((d, tn), lambda i, j: (0, j)),
            pl.BlockSpec((d, LANES), lambda i, j: (0, 0)),
        ],
        out_specs=[
            pl.BlockSpec((tm, tn), lambda i, j: (i, j)),
            pl.BlockSpec((tm, LANES), lambda i, j: (i, 0)),
        ],
        out_shape=[
            jax.ShapeDtypeStruct((rows, n), BF16),
            jax.ShapeDtypeStruct((rows, LANES), F32),
        ],
        scratch_shapes=[pltpu.VMEM((tm, d), BF16)],
        compiler_params=_cparams(("parallel", "arbitrary")),
        name="in_proj",
    )(x, norm_g.reshape(1, d), mods, mods, w_main, w_gate)


def _proj_res_kernel(*refs, n_in):
    ys, ws = refs[:n_in], refs[n_in:2 * n_in]
    x_ref, g_ref, o_ref = refs[2 * n_in:]
    acc = _dot(ys[0][...], ws[0][...])
    for y, w in zip(ys[1:], ws[1:]):
        acc = acc + _dot(y[...], w[...])
    o_ref[...] = x_ref[...] + g_ref[...] * acc


def _proj_res(ys, ws, x, mods, layer, k_gate, mrow, tm, tn):
    rows, d = x.shape
    tm = min(tm, rows)
    n_in = len(ys)
    in_specs = [pl.BlockSpec((tm, y.shape[1]), lambda i, j: (i, 0)) for y in ys]
    in_specs += [pl.BlockSpec((w.shape[0], tn), lambda i, j: (0, j)) for w in ws]
    in_specs += [pl.BlockSpec((tm, tn), lambda i, j: (i, j)), _mod_spec(layer, k_gate, mrow, d, tn)]
    return pl.pallas_call(
        functools.partial(_proj_res_kernel, n_in=n_in),
        grid=(rows // tm, d // tn),
        in_specs=in_specs,
        out_specs=pl.BlockSpec((tm, tn), lambda i, j: (i, j)),
        out_shape=jax.ShapeDtypeStruct((rows, d), F32),
        compiler_params=_cparams(("parallel", "parallel")),
        name="proj_res",
    )(*ys, *ws, x, mods)


FFN_HALO = 16


def _ffn_up_kernel(xp_ref, x_ref, xn_ref, g_ref, sh_ref, sc_ref, wg_ref, wu_ref,
                   cwg_ref, cwu_ref, cbg_ref, cbu_ref, o_ref, h_scr, *, seq_len, rc):
    tm = x_ref.shape[0]
    i = pl.program_id(0)

    @pl.when(pl.program_id(1) == 0)
    def _():
        def norm(x):
            return _rms_mod(x, g_ref[...], sh_ref[...], sc_ref[...]).astype(BF16)

        h_scr[0:FFN_HALO, :] = norm(xp_ref[...])
        h_scr[FFN_HALO + tm:2 * FFN_HALO + tm, :] = norm(xn_ref[...])

        def body(r, carry):
            src = pl.ds(pl.multiple_of(r * rc, rc), rc)
            dst = pl.ds(pl.multiple_of(FFN_HALO + r * rc, FFN_HALO), rc)
            h_scr[dst, :] = norm(x_ref[src, :])
            return carry

        lax.fori_loop(0, tm // rc, body, 0)

    h = h_scr[...]
    pos = lax.rem(lax.broadcasted_iota(jnp.int32, (tm, 1), 0) + i * tm, seq_len)
    has_prev = pos != 0
    has_next = pos != seq_len - 1

    def conv_half(w_ref, cw_ref, cb_ref):
        u = _dot(h, w_ref[...])
        up = jnp.where(has_prev, u[FFN_HALO - 1:FFN_HALO - 1 + tm], 0.0)
        uc = u[FFN_HALO:FFN_HALO + tm]
        un = jnp.where(has_next, u[FFN_HALO + 1:FFN_HALO + 1 + tm], 0.0)
        cw = cw_ref[...]
        return up * cw[0:1] + uc * cw[1:2] + un * cw[2:3] + cb_ref[...]

    yg = conv_half(wg_ref, cwg_ref, cbg_ref)
    yu = conv_half(wu_ref, cwu_ref, cbu_ref)
    o_ref[...] = (_silu(yg) * yu).astype(o_ref.dtype)


def _ffn_up(x, mods, layer, mrow, norm_g, w_gate, w_up, cw_g, cw_u, cb_g, cb_u, seq_len, tm, tn=512):
    rows, d = x.shape
    fp = w_gate.shape[1]
    tm = min(tm, rows)
    hb = tm // FFN_HALO
    last = rows // FFN_HALO - 1
    return pl.pallas_call(
        functools.partial(_ffn_up_kernel, seq_len=seq_len, rc=min(256, tm)),
        grid=(rows // tm, fp // tn),
        in_specs=[
            pl.BlockSpec((FFN_HALO, d), lambda i, j: (jnp.maximum(i * hb - 1, 0), 0)),
            pl.BlockSpec((tm, d), lambda i, j: (i, 0)),
            pl.BlockSpec((FFN_HALO, d), lambda i, j: (jnp.minimum((i + 1) * hb, last), 0)),
            pl.BlockSpec((1, d), lambda i, j: (0, 0)),
            _mod_spec(layer, 3, mrow, d),
            _mod_spec(layer, 4, mrow, d),
            pl.BlockSpec((d, tn), lambda i, j: (0, j)),
            pl.BlockSpec((d, tn), lambda i, j: (0, j)),
            pl.BlockSpec((3, tn), lambda i, j: (0, j)),
            pl.BlockSpec((3, tn), lambda i, j: (0, j)),
            pl.BlockSpec((1, tn), lambda i, j: (0, j)),
            pl.BlockSpec((1, tn), lambda i, j: (0, j)),
        ],
        out_specs=pl.BlockSpec((tm, tn), lambda i, j: (i, j)),
        out_shape=jax.ShapeDtypeStruct((rows, fp), BF16),
        scratch_shapes=[pltpu.VMEM((tm + 2 * FFN_HALO, d), BF16)],
        compiler_params=_cparams(("parallel", "arbitrary")),
        name="ffn_up",
    )(x, x, x, norm_g.reshape(1, d), mods, mods, w_gate, w_up, cw_g, cw_u, cb_g, cb_u)


def _fill_padded(xpad_ref, x_ref):
    ts = x_ref.shape[0]
    zeros = jnp.zeros((CONV_PAD, xpad_ref.shape[1]), F32)
    xpad_ref[0:CONV_PAD, :] = zeros
    xpad_ref[CONV_PAD + ts:2 * CONV_PAD + ts, :] = zeros
    rc = min(512, ts)

    def body(r, carry):
        src = pl.ds(pl.multiple_of(r * rc, rc), rc)
        dst = pl.ds(pl.multiple_of(CONV_PAD + r * rc, CONV_PAD), rc)
        xpad_ref[dst, :] = x_ref[src, :].astype(F32)
        return carry

    lax.fori_loop(0, ts // rc, body, 0)


def _conv4_rows(xpad_ref, start, rc, cw):
    win = xpad_ref[pl.ds(pl.multiple_of(start, CONV_PAD), rc + 2 * CONV_PAD), :]
    return (win[CONV_PAD - 2:CONV_PAD - 2 + rc] * cw[0:1] + win[CONV_PAD - 1:CONV_PAD - 1 + rc] * cw[1:2]
            + win[CONV_PAD:CONV_PAD + rc] * cw[2:3] + win[CONV_PAD + 1:CONV_PAD + 1 + rc] * cw[3:4])


def _gdn_prep_kernel(x_ref, cw_ref, o_ref, xpad, *, n_heads, rc):
    ts = x_ref.shape[0]
    j = pl.program_id(1)
    _fill_padded(xpad, x_ref)
    cw = cw_ref[...]
    is_qk = j < 2 * n_heads
    fac = jnp.where(j < n_heads, HD ** -0.5, 1.0).astype(F32)

    def body(r, carry):
        start = r * rc
        y = _silu(_conv4_rows(xpad, start, rc, cw))
        yn = y * lax.rsqrt(jnp.sum(y * y, axis=-1, keepdims=True) + EPS) * fac
        o_ref[pl.ds(pl.multiple_of(start, rc), rc), :] = jnp.where(is_qk, yn, y).astype(o_ref.dtype)
        return carry

    lax.fori_loop(0, ts // rc, body, 0)


def _gdn_prep(p, conv_w, dims, ts):
    rows = p.shape[0]
    nb = 3 * dims.gdn_heads
    return pl.pallas_call(
        functools.partial(_gdn_prep_kernel, n_heads=dims.gdn_heads, rc=min(256, ts)),
        grid=(rows // ts, nb),
        in_specs=[
            pl.BlockSpec((ts, HD), lambda b, j: (b, j)),
            pl.BlockSpec((GDN_CONV, HD), lambda b, j: (0, j)),
        ],
        out_specs=pl.BlockSpec((ts, HD), lambda b, j: (b, j)),
        out_shape=jax.ShapeDtypeStruct((rows, nb * HD), BF16),
        scratch_shapes=[pltpu.VMEM((ts + 2 * CONV_PAD, HD), F32)],
        compiler_params=_cparams(("parallel", "parallel")),
        name="gdn_prep",
    )(p, conv_w)


def _gdn_masks(n_heads, reverse):
    n = n_heads * CH
    r = np.arange(n)[:, None]
    c = np.arange(n)[None, :]
    same = (r // CH) == (c // CH)
    after = (r <= c) if reverse else (r >= c)
    incl = same & after
    strict = incl & (r != c)
    masks = [incl, strict, same, r == c]
    s = 1
    while s < CH:
        pair = (r // (2 * s)) == (c // (2 * s))
        hi, lo = (r // s) % 2 == 1, (c // s) % 2 == 0
        if reverse:
            hi, lo = (r // s) % 2 == 0, (c // s) % 2 == 1
        masks.append(pair & hi & lo & same)
        s *= 2
    return np.stack(masks).astype(np.float32)


def _gdn_scan_kernel(q_ref, k_ref, v_ref, g_ref, alog_ref, dt_ref, m_ref, s0_ref,
                     o_ref, sfin_ref, s_scr, la_scr, be_scr, *, n_heads, reverse, d_idx):
    seg = q_ref.shape[0]
    nch = seg // CH
    n = n_heads * CH
    sidx = pl.program_id(1)

    @pl.when(sidx == 0)
    def _():
        s_scr[...] = s0_ref[...]

    g_all = g_ref[...]
    la_scr[...] = -jnp.exp(alog_ref[...]) * jax.nn.softplus(g_all + dt_ref[...])
    be_scr[...] = jax.nn.sigmoid(g_all)

    rows = lax.broadcasted_iota(jnp.int32, (LANES, n_heads * LANES), 0)
    head = lax.broadcasted_iota(jnp.int32, (LANES, n_heads * LANES), 1) // LANES
    sel_be = (rows == d_idx * n_heads + head).astype(F32)
    sel_la = (rows == (2 + d_idx) * n_heads + head).astype(F32)

    incl, strict, same, eye = m_ref[0], m_ref[1], m_ref[2], m_ref[3]

    def stack(x):
        return jnp.concatenate([x[:, h * LANES:(h + 1) * LANES] for h in range(n_heads)], axis=0)

    def body(c, carry):
        ci = (nch - 1 - c) if reverse else c
        rs = pl.ds(pl.multiple_of(ci * CH, CH), CH)
        la = stack(_dot_exact(la_scr[rs, :], sel_la))
        be = stack(_dot_exact(be_scr[rs, :], sel_be))
        g = _dot_exact(incl, la)
        gl = _dot_exact(same, la)
        eg = jnp.exp(g)
        g2 = jnp.concatenate([g] * (n // LANES), axis=1) if n >= LANES else g[:, :n]
        grow = jnp.sum(g2 * eye, axis=0, keepdims=True)
        dm = jnp.exp(jnp.minimum(g2 - grow, 0.0)) * incl
        qs = stack(q_ref[rs, :]).astype(F32)
        ks = stack(k_ref[rs, :]).astype(F32)
        vs = stack(v_ref[rs, :]).astype(F32)
        kb = ks * be
        ksb = ks.astype(BF16)
        kk = _dot_nt(kb.astype(BF16), ksb)
        qk = _dot_nt(qs.astype(BF16), ksb)
        lm = kk * dm * strict
        attn = qk * dm
        x = eye - lm * m_ref[4]
        lvl = 5
        s = 2
        while s < CH:
            a_off = lm * m_ref[lvl]
            x = x - _dot_exact(_dot_exact(x, a_off), x)
            lvl += 1
            s *= 2
        rhs = jnp.concatenate([vs * be, kb * eg], axis=1)
        sol = _dot_exact(x, rhs)
        u, w = sol[:, :HD], sol[:, HD:]
        qd = (qs * eg).astype(BF16)
        kd = (ks * jnp.exp(gl - g)).astype(BF16)
        wb = w.astype(BF16)
        ws_parts, qs_parts = [], []
        for h in range(n_heads):
            sh = s_scr[h].astype(BF16)
            hs = slice(h * CH, (h + 1) * CH)
            ws_parts.append(_dot(wb[hs], sh))
            qs_parts.append(_dot(qd[hs], sh))
        vnew = u - jnp.concatenate(ws_parts, axis=0)
        vnb = vnew.astype(BF16)
        o = jnp.concatenate(qs_parts, axis=0) + _dot(attn.astype(BF16), vnb)
        for h in range(n_heads):
            hs = slice(h * CH, (h + 1) * CH)
            decay = jnp.exp(gl[h * CH:h * CH + 1, :])
            s_scr[h] = s_scr[h] * decay + _dot_tn(kd[hs], vnb[hs])
            o_ref[rs, h * HD:(h + 1) * HD] = o[hs]
        return carry

    lax.fori_loop(0, nch, body, 0)

    @pl.when(sidx == pl.num_programs(1) - 1)
    def _():
        sfin_ref[...] = s_scr[...]


def _gdn_scan(qkv, gates, alog_row, dt_row, s0, dims, ts, reverse, d_idx):
    rows = qkv.shape[0]
    nh = dims.gdn_heads
    w = nh * HD
    seg = min(1024, ts)
    nseg = ts // seg
    masks = jnp.asarray(_gdn_masks(nh, reverse))

    def rb(b, s):
        return b * nseg + ((nseg - 1 - s) if reverse else s)

    return pl.pallas_call(
        functools.partial(_gdn_scan_kernel, n_heads=nh, reverse=reverse, d_idx=d_idx),
        grid=(rows // ts, nseg),
        in_specs=[
            pl.BlockSpec((seg, w), lambda b, s: (rb(b, s), 0)),
            pl.BlockSpec((seg, w), lambda b, s: (rb(b, s), 1)),
            pl.BlockSpec((seg, w), lambda b, s: (rb(b, s), 2)),
            pl.BlockSpec((seg, LANES), lambda b, s: (rb(b, s), 0)),
            pl.BlockSpec((1, LANES), lambda b, s: (0, 0)),
            pl.BlockSpec((1, LANES), lambda b, s: (0, 0)),
            pl.BlockSpec(masks.shape, lambda b, s: (0, 0, 0)),
            pl.BlockSpec((None, nh, HD, HD), lambda b, s: (b, 0, 0, 0)),
        ],
        out_specs=[
            pl.BlockSpec((seg, w), lambda b, s: (rb(b, s), 0)),
            pl.BlockSpec((None, nh, HD, HD), lambda b, s: (b, 0, 0, 0)),
        ],
        out_shape=[
            jax.ShapeDtypeStruct((rows, w), F32),
            jax.ShapeDtypeStruct(s0.shape, F32),
        ],
        scratch_shapes=[
            pltpu.VMEM((nh, HD, HD), F32),
            pltpu.VMEM((seg, LANES), F32),
            pltpu.VMEM((seg, LANES), F32),
        ],
        compiler_params=_cparams(("parallel", "arbitrary")),
        name="gdn_scan_rev" if reverse else "gdn_scan_fwd",
    )(qkv, qkv, qkv, gates, alog_row, dt_row, masks, s0)


def _gdn_out_kernel(of_ref, ob_ref, z_ref, g_ref, y_ref, *, n_heads):
    g = g_ref[...]
    for h in range(n_heads):
        cs = slice(h * HD, (h + 1) * HD)
        o = of_ref[:, cs] + ob_ref[:, cs]
        y = o * lax.rsqrt(jnp.mean(o * o, axis=-1, keepdims=True) + EPS) * g
        y_ref[:, cs] = (y * _silu(z_ref[:, cs].astype(F32))).astype(y_ref.dtype)


def _gdn_out(o_f, o_b, p, norm_g, dims, tr=512):
    rows, w = o_f.shape
    tr = min(tr, rows)
    zb = dims.off_z // w
    return pl.pallas_call(
        functools.partial(_gdn_out_kernel, n_heads=dims.gdn_heads),
        grid=(rows // tr,),
        in_specs=[
            pl.BlockSpec((tr, w), lambda i: (i, 0)),
            pl.BlockSpec((tr, w), lambda i: (i, 0)),
            pl.BlockSpec((tr, w), lambda i: (i, zb)),
            pl.BlockSpec((1, HD), lambda i: (0, 0)),
        ],
        out_specs=pl.BlockSpec((tr, w), lambda i: (i, 0)),
        out_shape=jax.ShapeDtypeStruct((rows, w), BF16),
        compiler_params=_cparams(("parallel",)),
        name="gdn_out",
    )(o_f, o_b, p, norm_g.reshape(1, HD))


def _qk_prep_kernel(q_ref, k_ref, qg_ref, kg_ref, cos_ref, sin_ref, qo_ref, ko_ref, *, rope):
    if rope:
        cos, sin = cos_ref[...], sin_ref[...]
        first_half = (lax.broadcasted_iota(jnp.int32, cos.shape, 1) % 64) < 32

    def one(x, g, scale):
        x = x.astype(F32)
        y = x * lax.rsqrt(jnp.mean(x * x, axis=-1, keepdims=True) + EPS) * g
        if rope:
            partner = jnp.where(first_half, pltpu.roll(y, 96, 1), pltpu.roll(y, 32, 1))
            y = y * cos + partner * sin
        return y * scale

    for h in range(q_ref.shape[1] // HD):
        cs = slice(h * HD, (h + 1) * HD)
        qo_ref[:, cs] = one(q_ref[:, cs], qg_ref[...], HD ** -0.5).astype(qo_ref.dtype)
    for h in range(k_ref.shape[1] // HD):
        cs = slice(h * HD, (h + 1) * HD)
        ko_ref[:, cs] = one(k_ref[:, cs], kg_ref[...], 1.0).astype(ko_ref.dtype)


def _qk_prep(p, q_g, k_g, cos, sin, dims, rope, tr=512):
    rows = p.shape[0]
    wq, wk = dims.w_gqa, dims.kv_heads * HD
    tr = min(tr, rows)
    nt = cos.shape[0] // tr if rope else 1
    tab = pl.BlockSpec((tr, HD), (lambda i: (i % nt, 0)) if rope else (lambda i: (0, 0)))
    return pl.pallas_call(
        functools.partial(_qk_prep_kernel, rope=rope),
        grid=(rows // tr,),
        in_specs=[
            pl.BlockSpec((tr, wq), lambda i: (i, dims.off_q // wq)),
            pl.BlockSpec((tr, wk), lambda i: (i, dims.off_k // wk)),
            pl.BlockSpec((1, HD), lambda i: (0, 0)),
            pl.BlockSpec((1, HD), lambda i: (0, 0)),
            tab, tab,
        ],
        out_specs=[
            pl.BlockSpec((tr, wq), lambda i: (i, 0)),
            pl.BlockSpec((tr, wk), lambda i: (i, 0)),
        ],
        out_shape=[
            jax.ShapeDtypeStruct((rows, wq), BF16),
            jax.ShapeDtypeStruct((rows, wk), BF16),
        ],
        compiler_params=_cparams(("parallel",)),
        name="qk_prep",
    )(p, p, q_g.reshape(1, HD), k_g.reshape(1, HD), cos, sin)


def _attn_kernel(*refs, n_kv, tk, hpg):
    q_ref, kv, o_ref = refs[0], refs[1:1 + 2 * n_kv], refs[1 + 2 * n_kv]
    tq = q_ref.shape[0]
    q = jnp.concatenate([q_ref[:, h * HD:(h + 1) * HD] for h in range(hpg)], axis=0)
    m_rows = hpg * tq

    def tile(k, v, carry):
        m, l, acc = carry
        s = _dot_nt(q, k)
        m_new = jnp.maximum(m, jnp.max(s, axis=-1, keepdims=True))
        alpha = jnp.exp(m - m_new)
        p = jnp.exp(s - m_new)
        l = alpha * l + jnp.sum(p, axis=-1, keepdims=True)
        acc = alpha * acc + _dot(p.astype(BF16), v)
        return m_new, l, acc

    carry = (jnp.full((m_rows, 1), -jnp.inf, F32), jnp.zeros((m_rows, 1), F32), jnp.zeros((m_rows, HD), F32))
    for sidx in range(n_kv):
        k_ref, v_ref = kv[2 * sidx], kv[2 * sidx + 1]
        n_rows = k_ref.shape[0]
        t = min(tk, n_rows)

        def body(it, c, k_ref=k_ref, v_ref=v_ref, t=t):
            rs = pl.ds(pl.multiple_of(it * t, t), t)
            return tile(k_ref[rs, :], v_ref[rs, :], c)

        carry = lax.fori_loop(0, n_rows // t, body, carry)
    _, l, acc = carry
    out = acc / l
    for h in range(hpg):
        o_ref[:, h * HD:(h + 1) * HD] = out[h * tq:(h + 1) * tq].astype(o_ref.dtype)


def _attention(q, kvs, dims, tq_rows, tq=256, tk=512):
    rows = q.shape[0]
    hpg = dims.q_heads // dims.kv_heads
    gw = hpg * HD
    tq = min(tq, tq_rows)
    nqb = tq_rows // tq
    in_specs = [pl.BlockSpec((tq, gw), lambda b, g, i: (b * nqb + i, g))]
    args = [q]
    for k, v_src, v_blk, s_len in kvs:
        in_specs.append(pl.BlockSpec((s_len, HD), lambda b, g, i: (b, g)))
        in_specs.append(pl.BlockSpec((s_len, HD), lambda b, g, i, v_blk=v_blk: (b, v_blk + g)))
        args += [k, v_src]
    return pl.pallas_call(
        functools.partial(_attn_kernel, n_kv=len(kvs), tk=tk, hpg=hpg),
        grid=(rows // tq_rows, dims.kv_heads, nqb),
        in_specs=in_specs,
        out_specs=pl.BlockSpec((tq, gw), lambda b, g, i: (b * nqb + i, g)),
        out_shape=jax.ShapeDtypeStruct((rows, dims.w_gqa), BF16),
        compiler_params=_cparams(("parallel", "parallel", "arbitrary")),
        name="attention",
    )(*args)


N_SEG = 8


def _lru_kernel(x_ref, gb_ref, cw_ref, cb_ref, wg_ref, bg_ref, lam_ref, h0_ref,
                y_ref, hlast_ref, xpad, a_f, u_f, a_b, u_b, *, rc):
    ts = x_ref.shape[0]
    sl = ts // N_SEG
    _fill_padded(xpad, x_ref)
    cw, cb = cw_ref[...], cb_ref[...]
    wg, bg = wg_ref[...], bg_ref[...]
    sp = jax.nn.softplus(-lam_ref[...])

    def gates_body(r, carry):
        start = r * rc
        rows = pl.ds(pl.multiple_of(start, rc), rc)
        xc = _conv4_rows(xpad, start, rc, cw) + cb
        gts = jax.nn.sigmoid(_dot(xc.astype(BF16), wg) + bg)
        for d, (a_ref, u_ref) in enumerate(((a_f, u_f), (a_b, u_b))):
            rg = gts[:, (2 * d) * LANES:(2 * d + 1) * LANES]
            ig = gts[:, (2 * d + 1) * LANES:(2 * d + 2) * LANES]
            log_a = -LRU_C * rg * sp[d:d + 1]
            th = jnp.tanh(log_a)
            a_ref[rows, :] = jnp.exp(log_a)
            u_ref[rows, :] = jnp.sqrt(-2.0 * th / (1.0 - th)) * (ig * xc)
        return carry

    lax.fori_loop(0, ts // rc, gates_body, 0)

    zeros = jnp.zeros((N_SEG, LANES), F32)
    ones = jnp.ones((N_SEG, LANES), F32)

    def scan_dir(a_ref, u_ref, reverse):
        def body(t, carry):
            h, acc = carry
            tt = (sl - 1 - t) if reverse else t
            idx = pl.ds(tt, N_SEG, stride=sl)
            a = a_ref[idx, :]
            h = a * h + u_ref[idx, :]
            acc = a * acc
            u_ref[idx, :] = h
            a_ref[idx, :] = acc
            return h, acc

        return lax.fori_loop(0, sl, body, (zeros, ones))

    hf, af = scan_dir(a_f, u_f, False)
    hb, ab = scan_dir(a_b, u_b, True)

    h0 = h0_ref[...]
    carry_f, carry_b = [h0[0:1]], [h0[1:2]]
    for s in range(N_SEG):
        carry_f.append(hf[s:s + 1] + af[s:s + 1] * carry_f[-1])
        sb = N_SEG - 1 - s
        carry_b.append(hb[sb:sb + 1] + ab[sb:sb + 1] * carry_b[-1])
    hlast_ref[...] = jnp.concatenate([carry_f[-1], carry_b[-1], jnp.zeros((6, LANES), F32)], axis=0)

    rc2 = min(rc, sl)
    for s in range(N_SEG):
        cf, cbk = carry_f[s], carry_b[N_SEG - 1 - s]

        def out_body(r, carry, s=s, cf=cf, cbk=cbk):
            rows = pl.ds(pl.multiple_of(s * sl + r * rc2, rc2), rc2)
            h = (u_f[rows, :] + a_f[rows, :] * cf) + (u_b[rows, :] + a_b[rows, :] * cbk)
            y_ref[rows, :] = (jax.nn.gelu(gb_ref[rows, :].astype(F32)) * h).astype(y_ref.dtype)
            return carry

        lax.fori_loop(0, sl // rc2, out_body, 0)


def _lru(p, conv_w, conv_b, w_gates, b_gates, lam, h0, dims, ts):
    rows = p.shape[0]
    w = dims.w_lru
    nct = w // LANES
    xb, gb = dims.off_lx // LANES, dims.off_lg // LANES
    nb = rows // ts
    return pl.pallas_call(
        functools.partial(_lru_kernel, rc=min(256, ts)),
        grid=(nb, nct),
        in_specs=[
            pl.BlockSpec((ts, LANES), lambda b, c: (b, xb + c)),
            pl.BlockSpec((ts, LANES), lambda b, c: (b, gb + c)),
            pl.BlockSpec((LRU_CONV, LANES), lambda b, c: (0, c)),
            pl.BlockSpec((1, LANES), lambda b, c: (0, c)),
            pl.BlockSpec((None, LANES, 4 * LANES), lambda b, c: (c, 0, 0)),
            pl.BlockSpec((None, 1, 4 * LANES), lambda b, c: (c, 0, 0)),
            pl.BlockSpec((2, LANES), lambda b, c: (0, c)),
            pl.BlockSpec((None, 8, LANES), lambda b, c: (b, 0, c)),
        ],
        out_specs=[
            pl.BlockSpec((ts, LANES), lambda b, c: (b, c)),
            pl.BlockSpec((None, 8, LANES), lambda b, c: (b, 0, c)),
        ],
        out_shape=[
            jax.ShapeDtypeStruct((rows, w), BF16),
            jax.ShapeDtypeStruct((nb, 8, w), F32),
        ],
        scratch_shapes=[pltpu.VMEM((ts + 2 * CONV_PAD, LANES), F32)] + [pltpu.VMEM((ts, LANES), F32)] * 4,
        compiler_params=_cparams(("parallel", "parallel")),
        name="lru",
    )(p, p, conv_w, conv_b, w_gates, b_gates, lam, h0)


def _final_norm_kernel(x_ref, g_ref, o_ref):
    x = x_ref[...]
    o_ref[...] = x * lax.rsqrt(jnp.mean(x * x, axis=-1, keepdims=True) + EPS) * g_ref[...]


def _final_norm(x, g, tr=512):
    rows, d = x.shape
    tr = min(tr, rows)
    return pl.pallas_call(
        _final_norm_kernel,
        grid=(rows // tr,),
        in_specs=[pl.BlockSpec((tr, d), lambda i: (i, 0)), pl.BlockSpec((1, d), lambda i: (0, 0))],
        out_specs=pl.BlockSpec((tr, d), lambda i: (i, 0)),
        out_shape=jax.ShapeDtypeStruct((rows, d), F32),
        compiler_params=_cparams(("parallel",)),
        name="final_norm",
    )(x, g.reshape(1, d))


def _rope_tables(dims):
    t = dims.seq
    rows = t // dims.grid_w
    row = jnp.repeat(jnp.arange(rows, dtype=F32), dims.grid_w)
    col = jnp.tile(jnp.arange(dims.grid_w, dtype=F32), rows)
    axis_dim = HD // 2
    inv_freq = ROPE_THETA ** (-jnp.arange(0, axis_dim, 2, dtype=F32) / axis_dim)
    ar, ac = row[:, None] * inv_freq, col[:, None] * inv_freq
    cos = jnp.concatenate([jnp.cos(ar)] * 2 + [jnp.cos(ac)] * 2, axis=-1)
    sin = jnp.concatenate([-jnp.sin(ar), jnp.sin(ar), -jnp.sin(ac), jnp.sin(ac)], axis=-1)
    return cos, sin


def _split_w_in(w_in, dims):
    nh = dims.gdn_heads
    kv = dims.kv_heads * HD
    sizes = (3 * dims.w_gdn, dims.w_gdn, nh, nh, nh, nh, dims.w_gqa, kv, kv, dims.w_lru, dims.w_lru)
    bounds = np.cumsum(sizes)[:-1].tolist()
    qkv, z, b_f, b_b, a_f, a_b, gq, gk, gv, lx, lg = jnp.split(w_in, bounds, axis=-1)
    main = jnp.concatenate([qkv, z, gq, gk, gv, lx, lg], axis=-1).astype(BF16)
    pad = jnp.zeros(w_in.shape[:2] + (LANES - 4 * nh,), w_in.dtype)
    gate = jnp.concatenate([b_f, b_b, a_f, a_b, pad], axis=-1).astype(BF16)
    return main, gate


def _gate_rows(v, dims):
    nh = dims.gdn_heads
    flat = v.reshape(v.shape[0], 1, 2 * nh).astype(F32)
    return jnp.pad(flat, ((0, 0), (0, 0), (2 * nh, LANES - 4 * nh)))


def _lru_gate_weights(gate_w, gate_b, dims):
    depth = gate_w.shape[0]
    bs = dims.w_lru // LRU_BLOCKS
    per = LANES // bs
    nct = dims.w_lru // LANES
    w = gate_w.reshape(depth, 4, nct, per, bs, bs)
    eye = jnp.eye(per, dtype=gate_w.dtype)
    dense = jnp.einsum("lgcpde,pq->lcpdgqe", w, eye).reshape(depth, nct, LANES, 4 * LANES)
    bias = gate_b.reshape(depth, 4, nct, LANES).transpose(0, 2, 1, 3).reshape(depth, nct, 1, 4 * LANES)
    return dense.astype(BF16), bias.astype(F32)


def _forward(dims, x, c, ctx, c_ctx, ada_w, ada_b, norm1_g, norm2_g, w_in, gdn_conv_w, gdn_a_log, gdn_dt_bias,
             gdn_norm_g, q_norm_g, k_norm_g, lru_conv_w, lru_conv_b, lru_gate_w, lru_gate_b, lru_lambda,
             w_out, ffn_w_up, ffn_conv_w, ffn_conv_b, ffn_w_down, final_norm_g):
    d, b, t, ct, depth = dims.d, dims.batch, dims.seq, dims.ctx, dims.depth
    f, fp = dims.ffn, dims.ffn_pad
    nh = dims.gdn_heads
    tm = min(1024, t)

    xl = x.reshape(b * t, d)
    xc = ctx.reshape(b * ct, d)

    cond8 = jnp.concatenate([c, c_ctx[None, :], jnp.zeros((8 - b - 1, d), F32)], axis=0)
    mods = _adaln(cond8, ada_w, ada_b).reshape(depth, 8, N_MOD, 1, d)
    blocks_per_seq = t // tm
    mrow_l = lambda i: i // blocks_per_seq
    mrow_c = lambda i: b

    w_main, w_gate = _split_w_in(w_in, dims)
    alog_rows, dt_rows = _gate_rows(gdn_a_log, dims), _gate_rows(gdn_dt_bias, dims)
    lru_w, lru_b = _lru_gate_weights(lru_gate_w, lru_gate_b, dims)
    wo = w_out.astype(BF16)
    wo_parts = (wo[:, :dims.w_gdn], wo[:, dims.w_gdn:dims.w_gdn + dims.w_gqa], wo[:, dims.w_gdn + dims.w_gqa:])
    padc = ((0, 0), (0, 0), (0, fp - f))
    wu_g = jnp.pad(ffn_w_up[:, :, :f], padc).astype(BF16)
    wu_u = jnp.pad(ffn_w_up[:, :, f:], padc).astype(BF16)
    cw_g, cw_u = jnp.pad(ffn_conv_w[:, :, :f], padc), jnp.pad(ffn_conv_w[:, :, f:], padc)
    cb = ffn_conv_b[:, None, :]
    cb_g, cb_u = jnp.pad(cb[:, :, :f], padc), jnp.pad(cb[:, :, f:], padc)
    wd = jnp.pad(ffn_w_down, ((0, 0), (0, fp - f), (0, 0))).astype(BF16)
    cos, sin = _rope_tables(dims)
    v_blk = dims.off_v // HD

    s_zero = jnp.zeros((b, nh, HD, HD), F32)
    h_zero = jnp.zeros((b, 8, dims.w_lru), F32)

    for l in range(depth):
        ctx_out = l < depth - 1
        p_l, g_l = _in_proj(xl, mods, l, mrow_l, norm1_g[l], w_main[l], w_gate[l], tm)
        p_c, g_c = _in_proj(xc, mods, l, mrow_c, norm1_g[l], w_main[l], w_gate[l], tm)

        qkv_l = _gdn_prep(p_l, gdn_conv_w[l], dims, t)
        qkv_c = _gdn_prep(p_c, gdn_conv_w[l], dims, ct)
        o_l, o_c = [], []
        for di, rev in enumerate((False, True)):
            oc, sc = _gdn_scan(qkv_c, g_c, alog_rows[l], dt_rows[l], s_zero, dims, ct, rev, di)
            ol, _ = _gdn_scan(qkv_l, g_l, alog_rows[l], dt_rows[l], sc, dims, t, rev, di)
            o_l.append(ol)
            o_c.append(oc)
        ya_l = _gdn_out(o_l[0], o_l[1], p_l, gdn_norm_g[l], dims)

        q_l, k_l = _qk_prep(p_l, q_norm_g[l], k_norm_g[l], cos, sin, dims, True)
        q_c, k_c = _qk_prep(p_c, q_norm_g[l], k_norm_g[l], cos, sin, dims, False)
        yb_l = _attention(q_l, [(k_l, p_l, v_blk, t), (k_c, p_c, v_blk, ct)], dims, t)

        yc_c, h_c = _lru(p_c, lru_conv_w[l], lru_conv_b[l][None, :], lru_w[l], lru_b[l], lru_lambda[l], h_zero, dims, ct)
        yc_l, _ = _lru(p_l, lru_conv_w[l], lru_conv_b[l][None, :], lru_w[l], lru_b[l], lru_lambda[l], h_c, dims, t)

        ffn_args = (wu_g[l], wu_u[l], cw_g[l], cw_u[l], cb_g[l], cb_u[l])
        xl = _proj_res((ya_l, yb_l, yc_l), [w[l] for w in wo_parts], xl, mods, l, 2, mrow_l, tm, 1024)
        a_l = _ffn_up(xl, mods, l, mrow_l, norm2_g[l], *ffn_args, t, tm)
        xl = _proj_res((a_l,), (wd[l],), xl, mods, l, 5, mrow_l, tm, 256)
        if ctx_out:
            ya_c = _gdn_out(o_c[0], o_c[1], p_c, gdn_norm_g[l], dims)
            yb_c = _attention(q_c, [(k_c, p_c, v_blk, ct)], dims, ct)
            xc = _proj_res((ya_c, yb_c, yc_c), [w[l] for w in wo_parts], xc, mods, l, 2, mrow_c, tm, 1024)
            a_c = _ffn_up(xc, mods, l, mrow_c, norm2_g[l], *ffn_args, ct, tm)
            xc = _proj_res((a_c,), (wd[l],), xc, mods, l, 5, mrow_c, tm, 256)

    return _final_norm(xl, final_norm_g).reshape(b, t, d)


def kernel(x, c, ctx, c_ctx, ada_w, ada_b, norm1_g, norm2_g, w_in, gdn_conv_w, gdn_a_log, gdn_dt_bias, gdn_norm_g, q_norm_g, k_norm_g, lru_conv_w, lru_conv_b, lru_gate_w, lru_gate_b, lru_lambda, w_out, ffn_w_up, ffn_conv_w, ffn_conv_b, ffn_w_down, final_norm_g):
    b, t, d = x.shape
    dims = Dims(d=d, batch=b, seq=t, ctx=ctx.shape[1], depth=ada_w.shape[0], grid_w=64)
    return _forward(dims, x, c, ctx, c_ctx, ada_w, ada_b, norm1_g, norm2_g, w_in, gdn_conv_w, gdn_a_log,
                    gdn_dt_bias, gdn_norm_g, q_norm_g, k_norm_g, lru_conv_w, lru_conv_b, lru_gate_w,
                    lru_gate_b, lru_lambda, w_out, ffn_w_up, ffn_conv_w, ffn_conv_b, ffn_w_down, final_norm_g)
```

```python
import functools
import math
from typing import NamedTuple

import jax
import jax.numpy as jnp
import numpy as np
from jax import lax
from jax.experimental import pallas as pl
from jax.experimental.pallas import tpu as pltpu

F32 = jnp.float32
BF16 = jnp.bfloat16
EPS = 1e-6
HD = 128
LANES = 128
N_MOD = 6
GDN_CONV = 4
LRU_CONV = 4
LRU_C = 8.0
LRU_BLOCKS = 8
ROPE_THETA = 10000.0
Q_SCALE = HD ** -0.5 * math.log2(math.e)
CH = 64
CONV_PAD = 8
VMEM_LIMIT = 56 * 1024 * 1024


class Dims(NamedTuple):
    d: int
    batch: int
    seq: int
    ctx: int
    depth: int
    grid_w: int

    @property
    def w_gdn(self):
        return self.d // 4

    @property
    def w_gqa(self):
        return self.d // 2

    @property
    def w_lru(self):
        return self.d - self.w_gdn - self.w_gqa

    @property
    def gdn_heads(self):
        return self.w_gdn // HD

    @property
    def q_heads(self):
        return self.w_gqa // HD

    @property
    def kv_heads(self):
        return self.q_heads // 4

    @property
    def ffn(self):
        return ((8 * self.d // 3 + 127) // 128) * 128

    @property
    def ffn_pad(self):
        return ((self.ffn + 511) // 512) * 512

    @property
    def p_cols(self):
        return 4 * self.w_gdn + self.w_gqa + 2 * self.kv_heads * HD + 2 * self.w_lru

    @property
    def off_z(self):
        return 3 * self.w_gdn

    @property
    def off_q(self):
        return 4 * self.w_gdn

    @property
    def off_k(self):
        return self.off_q + self.w_gqa

    @property
    def off_v(self):
        return self.off_k + self.kv_heads * HD

    @property
    def off_lx(self):
        return self.off_v + self.kv_heads * HD

    @property
    def off_lg(self):
        return self.off_lx + self.w_lru


def _cparams(sem, vmem=VMEM_LIMIT):
    return pltpu.CompilerParams(dimension_semantics=sem, vmem_limit_bytes=vmem)


def _dot(a, b):
    return jnp.dot(a, b, preferred_element_type=F32)


def _dot_exact(a, b):
    return jnp.dot(a, b, preferred_element_type=F32, precision=lax.Precision.HIGHEST)


def _dot_nt(a, b):
    return lax.dot_general(a, b, (((1,), (1,)), ((), ())), preferred_element_type=F32)


def _dot_tn(a, b):
    return lax.dot_general(a, b, (((0,), (0,)), ((), ())), preferred_element_type=F32)


def _silu(x):
    return x * jax.nn.sigmoid(x)


def _rms_mod(x, g, sh, sc):
    ms = jnp.mean(x * x, axis=-1, keepdims=True)
    return (x * lax.rsqrt(ms + EPS)) * g * (1.0 + sc) + sh


def _adaln_kernel(c_ref, w_ref, b_ref, o_ref):
    s = _silu(c_ref[...]).astype(BF16)
    o_ref[...] = _dot(s, w_ref[...].astype(BF16)) + b_ref[...]


def _adaln(cond8, ada_w, ada_b, tn=1024):
    depth, d, n = ada_w.shape
    tn = min(tn, n)
    return pl.pallas_call(
        _adaln_kernel,
        grid=(depth, n // tn),
        in_specs=[
            pl.BlockSpec((8, d), lambda l, j: (0, 0)),
            pl.BlockSpec((None, d, tn), lambda l, j: (l, 0, j)),
            pl.BlockSpec((None, 1, tn), lambda l, j: (l, 0, j)),
        ],
        out_specs=pl.BlockSpec((None, 8, tn), lambda l, j: (l, 0, j)),
        out_shape=jax.ShapeDtypeStruct((depth, 8, n), F32),
        compiler_params=_cparams(("parallel", "parallel")),
        name="adaln",
    )(cond8, ada_w, ada_b.reshape(depth, 1, n))


def _mod_spec(layer, k, mrow, d, tn=None):
    if tn is None:
        return pl.BlockSpec((None, None, None, 1, d), lambda i, j: (layer, mrow(i), k, 0, 0))
    return pl.BlockSpec((None, None, None, 1, tn), lambda i, j: (layer, mrow(i), k, 0, j))


def _in_proj_kernel(x_ref, g_ref, sh_ref, sc_ref, w_ref, wg_ref, p_ref, gate_ref, h_scr, *, rc):
    tm = x_ref.shape[0]

    @pl.when(pl.program_id(1) == 0)
    def _():
        def body(r, carry):
            rows = pl.ds(pl.multiple_of(r * rc, rc), rc)
            h = _rms_mod(x_ref[rows, :], g_ref[...], sh_ref[...], sc_ref[...]).astype(BF16)
            h_scr[rows, :] = h
            gate_ref[rows, :] = _dot(h, wg_ref[...])
            return carry

        lax.fori_loop(0, tm // rc, body, 0)

    p_ref[...] = _dot(h_scr[...], w_ref[...]).astype(p_ref.dtype)


def _in_proj(x, mods, layer, mrow, norm_g, w_main, w_gate, tm, tn=512):
    rows, d = x.shape
    n = w_main.shape[1]
    tm = min(tm, rows)
    assert n % tn == 0 and rows % tm == 0
    return pl.pallas_call(
        functools.partial(_in_proj_kernel, rc=min(256, tm)),
        grid=(rows // tm, n // tn),
        in_specs=[
            pl.BlockSpec((tm, d), lambda i, j: (i, 0)),
            pl.BlockSpec((1, d), lambda i, j: (0, 0)),
            _mod_spec(layer, 0, mrow, d),
            _mod_spec(layer, 1, mrow, d),
            pl.BlockSpec((d, tn), lambda i, j: (0, j)),
            pl.BlockSpec((d, LANES), lambda i, j: (0, 0)),
        ],
        out_specs=[
            pl.BlockSpec((tm, tn), lambda i, j: (i, j)),
            pl.BlockSpec((tm, LANES), lambda i, j: (i, 0)),
        ],
        out_shape=[
            jax.ShapeDtypeStruct((rows, n), BF16),
            jax.ShapeDtypeStruct((rows, LANES), F32),
        ],
        scratch_shapes=[pltpu.VMEM((tm, d), BF16)],
        compiler_params=_cparams(("parallel", "arbitrary")),
        name="in_proj",
    )(x, norm_g.reshape(1, d), mods, mods, w_main, w_gate)


def _proj_res_kernel(*refs, n_in):
    ys, ws = refs[:n_in], refs[n_in:2 * n_in]
    x_ref, g_ref, o_ref = refs[2 * n_in:]
    acc = _dot(ys[0][...], ws[0][...])
    for y, w in zip(ys[1:], ws[1:]):
        acc = acc + _dot(y[...], w[...])
    o_ref[...] = x_ref[...] + g_ref[...] * acc


def _proj_res(ys, ws, x, mods, layer, k_gate, mrow, tm, tn):
    rows, d = x.shape
    tm = min(tm, rows)
    n_in = len(ys)
    in_specs = [pl.BlockSpec((tm, y.shape[1]), lambda i, j: (i, 0)) for y in ys]
    in_specs += [pl.BlockSpec((w.shape[0], tn), lambda i, j: (0, j)) for w in ws]
    in_specs += [pl.BlockSpec((tm, tn), lambda i, j: (i, j)), _mod_spec(layer, k_gate, mrow, d, tn)]
    return pl.pallas_call(
        functools.partial(_proj_res_kernel, n_in=n_in),
        grid=(rows // tm, d // tn),
        in_specs=in_specs,
        out_specs=pl.BlockSpec((tm, tn), lambda i, j: (i, j)),
        out_shape=jax.ShapeDtypeStruct((rows, d), F32),
        compiler_params=_cparams(("parallel", "parallel")),
        name="proj_res",
    )(*ys, *ws, x, mods)


FFN_HALO = 16


def _ffn_up_kernel(xp_ref, x_ref, xn_ref, g_ref, sh_ref, sc_ref, wg_ref, wu_ref,
                   cwg_ref, cwu_ref, cbg_ref, cbu_ref, o_ref, h_scr, *, seq_len, rc):
    tm = x_ref.shape[0]
    i = pl.program_id(0)

    @pl.when(pl.program_id(1) == 0)
    def _():
        def norm(x):
            return _rms_mod(x, g_ref[...], sh_ref[...], sc_ref[...]).astype(BF16)

        h_scr[0:FFN_HALO, :] = norm(xp_ref[...])
        h_scr[FFN_HALO + tm:2 * FFN_HALO + tm, :] = norm(xn_ref[...])

        def body(r, carry):
            src = pl.ds(pl.multiple_of(r * rc, rc), rc)
            dst = pl.ds(pl.multiple_of(FFN_HALO + r * rc, FFN_HALO), rc)
            h_scr[dst, :] = norm(x_ref[src, :])
            return carry

        lax.fori_loop(0, tm // rc, body, 0)

    h = h_scr[...]
    pos = lax.rem(lax.broadcasted_iota(jnp.int32, (tm, 1), 0) + i * tm, seq_len)
    has_prev = pos != 0
    has_next = pos != seq_len - 1

    def conv_half(w_ref, cw_ref, cb_ref):
        u = _dot(h, w_ref[...])
        up = jnp.where(has_prev, u[FFN_HALO - 1:FFN_HALO - 1 + tm], 0.0)
        uc = u[FFN_HALO:FFN_HALO + tm]
        un = jnp.where(has_next, u[FFN_HALO + 1:FFN_HALO + 1 + tm], 0.0)
        cw = cw_ref[...]
        return up * cw[0:1] + uc * cw[1:2] + un * cw[2:3] + cb_ref[...]

    yg = conv_half(wg_ref, cwg_ref, cbg_ref)
    yu = conv_half(wu_ref, cwu_ref, cbu_ref)
    o_ref[...] = (_silu(yg) * yu).astype(o_ref.dtype)


def _ffn_up(x, mods, layer, mrow, norm_g, w_gate, w_up, cw_g, cw_u, cb_g, cb_u, seq_len, tm, tn=512):
    rows, d = x.shape
    fp = w_gate.shape[1]
    tm = min(tm, rows)
    hb = tm // FFN_HALO
    last = rows // FFN_HALO - 1
    return pl.pallas_call(
        functools.partial(_ffn_up_kernel, seq_len=seq_len, rc=min(256, tm)),
        grid=(rows // tm, fp // tn),
        in_specs=[
            pl.BlockSpec((FFN_HALO, d), lambda i, j: (jnp.maximum(i * hb - 1, 0), 0)),
            pl.BlockSpec((tm, d), lambda i, j: (i, 0)),
            pl.BlockSpec((FFN_HALO, d), lambda i, j: (jnp.minimum((i + 1) * hb, last), 0)),
            pl.BlockSpec((1, d), lambda i, j: (0, 0)),
            _mod_spec(layer, 3, mrow, d),
            _mod_spec(layer, 4, mrow, d),
            pl.BlockSpec((d, tn), lambda i, j: (0, j)),
            pl.BlockSpec((d, tn), lambda i, j: (0, j)),
            pl.BlockSpec((3, tn), lambda i, j: (0, j)),
            pl.BlockSpec((3, tn), lambda i, j: (0, j)),
            pl.BlockSpec((1, tn), lambda i, j: (0, j)),
            pl.BlockSpec((1, tn), lambda i, j: (0, j)),
        ],
        out_specs=pl.BlockSpec((tm, tn), lambda i, j: (i, j)),
        out_shape=jax.ShapeDtypeStruct((rows, fp), BF16),
        scratch_shapes=[pltpu.VMEM((tm + 2 * FFN_HALO, d), BF16)],
        compiler_params=_cparams(("parallel", "arbitrary")),
        name="ffn_up",
    )(x, x, x, norm_g.reshape(1, d), mods, mods, w_gate, w_up, cw_g, cw_u, cb_g, cb_u)


def _fill_padded(xpad_ref, x_ref):
    ts = x_ref.shape[0]
    zeros = jnp.zeros((CONV_PAD, xpad_ref.shape[1]), F32)
    xpad_ref[0:CONV_PAD, :] = zeros
    xpad_ref[CONV_PAD + ts:2 * CONV_PAD + ts, :] = zeros
    rc = min(512, ts)

    def body(r, carry):
        src = pl.ds(pl.multiple_of(r * rc, rc), rc)
        dst = pl.ds(pl.multiple_of(CONV_PAD + r * rc, CONV_PAD), rc)
        xpad_ref[dst, :] = x_ref[src, :].astype(F32)
        return carry

    lax.fori_loop(0, ts // rc, body, 0)


def _conv4_rows(xpad_ref, start, rc, cw):
    win = xpad_ref[pl.ds(pl.multiple_of(start, CONV_PAD), rc + 2 * CONV_PAD), :]
    return (win[CONV_PAD - 2:CONV_PAD - 2 + rc] * cw[0:1] + win[CONV_PAD - 1:CONV_PAD - 1 + rc] * cw[1:2]
            + win[CONV_PAD:CONV_PAD + rc] * cw[2:3] + win[CONV_PAD + 1:CONV_PAD + 1 + rc] * cw[3:4])


def _gdn_prep_kernel(x_ref, cw_ref, o_ref, xpad, *, n_heads, rc):
    ts = x_ref.shape[0]
    j = pl.program_id(1)
    _fill_padded(xpad, x_ref)
    cw = cw_ref[...]
    is_qk = j < 2 * n_heads
    fac = jnp.where(j < n_heads, HD ** -0.5, 1.0).astype(F32)

    def body(r, carry):
        start = r * rc
        y = _silu(_conv4_rows(xpad, start, rc, cw))
        yn = y * lax.rsqrt(jnp.sum(y * y, axis=-1, keepdims=True) + EPS) * fac
        o_ref[pl.ds(pl.multiple_of(start, rc), rc), :] = jnp.where(is_qk, yn, y).astype(o_ref.dtype)
        return carry

    lax.fori_loop(0, ts // rc, body, 0)


def _gdn_prep(p, conv_w, dims, ts):
    rows = p.shape[0]
    nb = 3 * dims.gdn_heads
    return pl.pallas_call(
        functools.partial(_gdn_prep_kernel, n_heads=dims.gdn_heads, rc=min(256, ts)),
        grid=(rows // ts, nb),
        in_specs=[
            pl.BlockSpec((ts, HD), lambda b, j: (b, j)),
            pl.BlockSpec((GDN_CONV, HD), lambda b, j: (0, j)),
        ],
        out_specs=pl.BlockSpec((ts, HD), lambda b, j: (b, j)),
        out_shape=jax.ShapeDtypeStruct((rows, nb * HD), BF16),
        scratch_shapes=[pltpu.VMEM((ts + 2 * CONV_PAD, HD), F32)],
        compiler_params=_cparams(("parallel", "parallel")),
        name="gdn_prep",
    )(p, conv_w)


def _gdn_masks(n_heads, reverse):
    n = n_heads * CH
    r = np.arange(n)[:, None]
    c = np.arange(n)[None, :]
    same = (r // CH) == (c // CH)
    after = (r <= c) if reverse else (r >= c)
    incl = same & after
    strict = incl & (r != c)
    masks = [incl, strict, same, r == c]
    s = 1
    while s < CH:
        pair = (r // (2 * s)) == (c // (2 * s))
        hi, lo = (r // s) % 2 == 1, (c // s) % 2 == 0
        if reverse:
            hi, lo = (r // s) % 2 == 0, (c // s) % 2 == 1
        masks.append(pair & hi & lo & same)
        s *= 2
    return np.stack(masks).astype(np.float32)


def _chunk_cumsum(x, reverse):
    seg = x.shape[0]
    pos = lax.broadcasted_iota(jnp.int32, x.shape, 0) % CH
    k = 1
    while k < CH:
        if reverse:
            x = x + jnp.where(pos < CH - k, pltpu.roll(x, seg - k, 0), 0.0)
        else:
            x = x + jnp.where(pos >= k, pltpu.roll(x, k, 0), 0.0)
        k *= 2
    return x


def _gdn_scan_kernel(q_ref, k_ref, v_ref, g_ref, alog_ref, dt_ref, m_ref, s0_ref,
                     o_ref, sfin_ref, s_scr, g_scr, be_scr, *, n_heads, reverse, d_idx, group):
    seg = q_ref.shape[0]
    nch = seg // CH
    n = n_heads * CH
    sidx = pl.program_id(1)

    @pl.when(sidx == 0)
    def _():
        s_scr[...] = s0_ref[...]

    g_all = g_ref[...]
    log_a = -jnp.exp(alog_ref[...]) * jax.nn.softplus(g_all + dt_ref[...])
    g_cum = _chunk_cumsum(log_a, reverse)
    beta = jax.nn.sigmoid(g_all)
    for h in range(n_heads):
        cb, cg = d_idx * n_heads + h, (2 + d_idx) * n_heads + h
        g_scr[:, h * LANES:(h + 1) * LANES] = jnp.broadcast_to(g_cum[:, cg:cg + 1], (seg, LANES))
        be_scr[:, h * LANES:(h + 1) * LANES] = jnp.broadcast_to(beta[:, cb:cb + 1], (seg, LANES))

    incl, strict, eye = m_ref[0], m_ref[1], m_ref[3]
    last = 0 if reverse else CH - 1

    def stack(x):
        return jnp.concatenate([x[:, h * LANES:(h + 1) * LANES] for h in range(n_heads)], axis=0)

    def body(t, carry):
        order = [t * group + j for j in range(group)]
        rss = [pl.ds(pl.multiple_of(((nch - 1 - c) if reverse else c) * CH, CH), CH) for c in order]
        gs = [stack(g_scr[rs, :]) for rs in rss]
        bes = [stack(be_scr[rs, :]) for rs in rss]
        gls = [jnp.concatenate([jnp.broadcast_to(g[h * CH + last:h * CH + last + 1], (CH, LANES))
                                for h in range(n_heads)], axis=0) for g in gs]
        egs = [jnp.exp(g) for g in gs]
        g2s = [jnp.concatenate([g] * (n // LANES), axis=1) if n >= LANES else g[:, :n] for g in gs]
        grows = [jnp.sum(g2 * eye, axis=0, keepdims=True) for g2 in g2s]
        dms = [jnp.exp(jnp.minimum(g2 - grow, 0.0)) * incl for g2, grow in zip(g2s, grows)]
        qss = [stack(q_ref[rs, :]) for rs in rss]
        kss = [stack(k_ref[rs, :]) for rs in rss]
        ksfs = [ks.astype(F32) for ks in kss]
        kbs = [ksf * be for ksf, be in zip(ksfs, bes)]
        grams = [_dot_nt(jnp.concatenate([kb.astype(BF16), qs], axis=0), ks) for kb, qs, ks in zip(kbs, qss, kss)]
        lms = [gram[:n] * dm * strict for gram, dm in zip(grams, dms)]
        attns = [(gram[n:] * dm).astype(BF16) for gram, dm in zip(grams, dms)]
        xs = [eye - lm * m_ref[4] for lm in lms]
        lvl = 5
        s = 2
        while s < CH:
            mask = m_ref[lvl]
            xbs = [x.astype(BF16) for x in xs]
            offs = [(lm * mask).astype(BF16) for lm in lms]
            tmp = [_dot(xb, off).astype(BF16) for xb, off in zip(xbs, offs)]
            xs = [x - _dot(t1, xb) for x, t1, xb in zip(xs, tmp, xbs)]
            lvl += 1
            s *= 2
        rhss = [jnp.concatenate([stack(v_ref[rs, :]).astype(F32) * be, kb * eg], axis=1)
                for rs, be, kb, eg in zip(rss, bes, kbs, egs)]
        sols = [rhs + _dot((x - eye).astype(BF16), rhs.astype(BF16)) for x, rhs in zip(xs, rhss)]
        qds = [(qs.astype(F32) * eg).astype(BF16) for qs, eg in zip(qss, egs)]
        kds = [(ksf * jnp.exp(gl - g)).astype(BF16) for ksf, gl, g in zip(ksfs, gls, gs)]
        for j in range(group):
            rs, gl, sol, qd, kd, attn = rss[j], gls[j], sols[j], qds[j], kds[j], attns[j]
            u, wb = sol[:, :HD], sol[:, HD:].astype(BF16)
            ws_parts, qs_parts = [], []
            for h in range(n_heads):
                sh = s_scr[h].astype(BF16)
                hs = slice(h * CH, (h + 1) * CH)
                both = _dot(jnp.concatenate([wb[hs], qd[hs]], axis=0), sh)
                ws_parts.append(both[:CH])
                qs_parts.append(both[CH:])
            vnew = u - jnp.concatenate(ws_parts, axis=0)
            vnb = vnew.astype(BF16)
            o = jnp.concatenate(qs_parts, axis=0) + _dot(attn, vnb)
            for h in range(n_heads):
                hs = slice(h * CH, (h + 1) * CH)
                decay = jnp.exp(gl[h * CH:h * CH + 1, :])
                s_scr[h] = s_scr[h] * decay + _dot_tn(kd[hs], vnb[hs])
                o_ref[rs, h * HD:(h + 1) * HD] = o[hs]
        return carry

    lax.fori_loop(0, nch // group, body, 0)

    @pl.when(sidx == pl.num_programs(1) - 1)
    def _():
        sfin_ref[...] = s_scr[...]


def _gdn_scan(qkv, gates, alog_row, dt_row, s0, dims, ts, reverse, d_idx):
    rows = qkv.shape[0]
    nh = dims.gdn_heads
    w = nh * HD
    seg = min(1024, ts)
    nseg = ts // seg
    masks = jnp.asarray(_gdn_masks(nh, reverse))

    def rb(b, s):
        return b * nseg + ((nseg - 1 - s) if reverse else s)

    return pl.pallas_call(
        functools.partial(_gdn_scan_kernel, n_heads=nh, reverse=reverse, d_idx=d_idx, group=math.gcd(4, seg // CH)),
        grid=(rows // ts, nseg),
        in_specs=[
            pl.BlockSpec((seg, w), lambda b, s: (rb(b, s), 0)),
            pl.BlockSpec((seg, w), lambda b, s: (rb(b, s), 1)),
            pl.BlockSpec((seg, w), lambda b, s: (rb(b, s), 2)),
            pl.BlockSpec((seg, LANES), lambda b, s: (rb(b, s), 0)),
            pl.BlockSpec((1, LANES), lambda b, s: (0, 0)),
            pl.BlockSpec((1, LANES), lambda b, s: (0, 0)),
            pl.BlockSpec(masks.shape, lambda b, s: (0, 0, 0)),
            pl.BlockSpec((None, nh, HD, HD), lambda b, s: (b, 0, 0, 0)),
        ],
        out_specs=[
            pl.BlockSpec((seg, w), lambda b, s: (rb(b, s), 0)),
            pl.BlockSpec((None, nh, HD, HD), lambda b, s: (b, 0, 0, 0)),
        ],
        out_shape=[
            jax.ShapeDtypeStruct((rows, w), F32),
            jax.ShapeDtypeStruct(s0.shape, F32),
        ],
        scratch_shapes=[
            pltpu.VMEM((nh, HD, HD), F32),
            pltpu.VMEM((seg, w), F32),
            pltpu.VMEM((seg, w), F32),
        ],
        compiler_params=_cparams(("parallel", "arbitrary")),
        name="gdn_scan_rev" if reverse else "gdn_scan_fwd",
    )(qkv, qkv, qkv, gates, alog_row, dt_row, masks, s0)


def _gdn_out_kernel(of_ref, ob_ref, z_ref, g_ref, y_ref, *, n_heads):
    g = g_ref[...]
    for h in range(n_heads):
        cs = slice(h * HD, (h + 1) * HD)
        o = of_ref[:, cs] + ob_ref[:, cs]
        y = o * lax.rsqrt(jnp.mean(o * o, axis=-1, keepdims=True) + EPS) * g
        y_ref[:, cs] = (y * _silu(z_ref[:, cs].astype(F32))).astype(y_ref.dtype)


def _gdn_out(o_f, o_b, p, norm_g, dims, tr=512):
    rows, w = o_f.shape
    tr = min(tr, rows)
    zb = dims.off_z // w
    return pl.pallas_call(
        functools.partial(_gdn_out_kernel, n_heads=dims.gdn_heads),
        grid=(rows // tr,),
        in_specs=[
            pl.BlockSpec((tr, w), lambda i: (i, 0)),
            pl.BlockSpec((tr, w), lambda i: (i, 0)),
            pl.BlockSpec((tr, w), lambda i: (i, zb)),
            pl.BlockSpec((1, HD), lambda i: (0, 0)),
        ],
        out_specs=pl.BlockSpec((tr, w), lambda i: (i, 0)),
        out_shape=jax.ShapeDtypeStruct((rows, w), BF16),
        compiler_params=_cparams(("parallel",)),
        name="gdn_out",
    )(o_f, o_b, p, norm_g.reshape(1, HD))


def _qk_prep_kernel(q_ref, k_ref, qg_ref, kg_ref, cos_ref, sin_ref, qo_ref, ko_ref, *, rope):
    if rope:
        cos, sin = cos_ref[...], sin_ref[...]
        first_half = (lax.broadcasted_iota(jnp.int32, cos.shape, 1) % 64) < 32

    def one(x, g, scale):
        x = x.astype(F32)
        y = x * lax.rsqrt(jnp.mean(x * x, axis=-1, keepdims=True) + EPS) * g
        if rope:
            partner = jnp.where(first_half, pltpu.roll(y, 96, 1), pltpu.roll(y, 32, 1))
            y = y * cos + partner * sin
        return y * scale

    for h in range(q_ref.shape[1] // HD):
        cs = slice(h * HD, (h + 1) * HD)
        qo_ref[:, cs] = one(q_ref[:, cs], qg_ref[...], Q_SCALE).astype(qo_ref.dtype)
    for h in range(k_ref.shape[1] // HD):
        cs = slice(h * HD, (h + 1) * HD)
        ko_ref[:, cs] = one(k_ref[:, cs], kg_ref[...], 1.0).astype(ko_ref.dtype)


def _qk_prep(p, q_g, k_g, cos, sin, dims, rope, tr=512):
    rows = p.shape[0]
    wq, wk = dims.w_gqa, dims.kv_heads * HD
    tr = min(tr, rows)
    nt = cos.shape[0] // tr if rope else 1
    tab = pl.BlockSpec((tr, HD), (lambda i: (i % nt, 0)) if rope else (lambda i: (0, 0)))
    return pl.pallas_call(
        functools.partial(_qk_prep_kernel, rope=rope),
        grid=(rows // tr,),
        in_specs=[
            pl.BlockSpec((tr, wq), lambda i: (i, dims.off_q // wq)),
            pl.BlockSpec((tr, wk), lambda i: (i, dims.off_k // wk)),
            pl.BlockSpec((1, HD), lambda i: (0, 0)),
            pl.BlockSpec((1, HD), lambda i: (0, 0)),
            tab, tab,
        ],
        out_specs=[
            pl.BlockSpec((tr, wq), lambda i: (i, 0)),
            pl.BlockSpec((tr, wk), lambda i: (i, 0)),
        ],
        out_shape=[
            jax.ShapeDtypeStruct((rows, wq), BF16),
            jax.ShapeDtypeStruct((rows, wk), BF16),
        ],
        compiler_params=_cparams(("parallel",)),
        name="qk_prep",
    )(p, p, q_g.reshape(1, HD), k_g.reshape(1, HD), cos, sin)


def _attn_kernel(*refs, n_kv, tk, hpg, n_part):
    q_ref, kv, o_ref = refs[0], refs[1:1 + 2 * n_kv], refs[1 + 2 * n_kv]
    tq = q_ref.shape[0]
    hpp = hpg // n_part
    qs = [jnp.concatenate([q_ref[:, h * HD:(h + 1) * HD] for h in range(pi * hpp, (pi + 1) * hpp)], axis=0)
          for pi in range(n_part)]
    m_rows = hpp * tq

    def tile(q, k, v1, carry):
        m, acc = carry
        s = _dot_nt(q, k)
        m_new = jnp.maximum(m, jnp.max(s, axis=-1, keepdims=True))
        p = jnp.exp2(s - m_new)
        acc = jnp.exp2(m - m_new) * acc + _dot(p.astype(BF16), v1)
        return m_new, acc

    carries = [(jnp.full((m_rows, 1), -jnp.inf, F32), jnp.zeros((m_rows, 2 * HD), F32)) for _ in range(n_part)]
    for sidx in range(n_kv):
        k_ref, v_ref = kv[2 * sidx], kv[2 * sidx + 1]
        n_rows = k_ref.shape[0]
        t = min(tk, n_rows)
        ones = jnp.ones((t, HD), BF16)

        def body(it, cs, k_ref=k_ref, v_ref=v_ref, t=t, ones=ones):
            rs = pl.ds(pl.multiple_of(it * t, t), t)
            k = k_ref[rs, :]
            v1 = jnp.concatenate([v_ref[rs, :], ones], axis=1)
            return [tile(q, k, v1, c) for q, c in zip(qs, cs)]

        n_it = n_rows // t
        carries = lax.fori_loop(0, n_it, body, carries, unroll=2 if n_it % 2 == 0 else 1)
    for pi, (_, acc) in enumerate(carries):
        out = acc[:, :HD] / acc[:, HD:]
        for hh in range(hpp):
            h = pi * hpp + hh
            o_ref[:, h * HD:(h + 1) * HD] = out[hh * tq:(hh + 1) * tq].astype(o_ref.dtype)


def _attention(q, kvs, dims, tq_rows, tq=256, tk=512):
    rows = q.shape[0]
    hpg = dims.q_heads // dims.kv_heads
    gw = hpg * HD
    tq = min(tq, tq_rows)
    nqb = tq_rows // tq
    in_specs = [pl.BlockSpec((tq, gw), lambda b, g, i: (b * nqb + i, g))]
    args = [q]
    for k, v_src, v_blk, s_len in kvs:
        in_specs.append(pl.BlockSpec((s_len, HD), lambda b, g, i: (b, g)))
        in_specs.append(pl.BlockSpec((s_len, HD), lambda b, g, i, v_blk=v_blk: (b, v_blk + g)))
        args += [k, v_src]
    return pl.pallas_call(
        functools.partial(_attn_kernel, n_kv=len(kvs), tk=tk, hpg=hpg, n_part=2 if hpg % 2 == 0 else 1),
        grid=(rows // tq_rows, dims.kv_heads, nqb),
        in_specs=in_specs,
        out_specs=pl.BlockSpec((tq, gw), lambda b, g, i: (b * nqb + i, g)),
        out_shape=jax.ShapeDtypeStruct((rows, dims.w_gqa), BF16),
        compiler_params=_cparams(("parallel", "parallel", "arbitrary")),
        name="attention",
    )(*args)


N_SEG = 8


def _lru_kernel(x_ref, gb_ref, cw_ref, cb_ref, wg_ref, bg_ref, lam_ref, h0_ref,
                y_ref, hlast_ref, xpad, a_f, u_f, a_b, u_b, *, rc):
    ts = x_ref.shape[0]
    sl = ts // N_SEG
    _fill_padded(xpad, x_ref)
    cw, cb = cw_ref[...], cb_ref[...]
    wg, bg = wg_ref[...], bg_ref[...]
    sp = jax.nn.softplus(-lam_ref[...])

    def gates_body(r, carry):
        start = r * rc
        rows = pl.ds(pl.multiple_of(start, rc), rc)
        xc = _conv4_rows(xpad, start, rc, cw) + cb
        gts = jax.nn.sigmoid(_dot(xc.astype(BF16), wg) + bg)
        for d, (a_ref, u_ref) in enumerate(((a_f, u_f), (a_b, u_b))):
            rg = gts[:, (2 * d) * LANES:(2 * d + 1) * LANES]
            ig = gts[:, (2 * d + 1) * LANES:(2 * d + 2) * LANES]
            log_a = -LRU_C * rg * sp[d:d + 1]
            th = jnp.tanh(log_a)
            a_ref[rows, :] = jnp.exp(log_a)
            u_ref[rows, :] = jnp.sqrt(-2.0 * th / (1.0 - th)) * (ig * xc)
        return carry

    lax.fori_loop(0, ts // rc, gates_body, 0)

    zeros = jnp.zeros((N_SEG, LANES), F32)
    ones = jnp.ones((N_SEG, LANES), F32)

    def scan_step(a_ref, u_ref, tt, h, acc):
        idx = pl.ds(tt, N_SEG, stride=sl)
        a = a_ref[idx, :]
        h = a * h + u_ref[idx, :]
        acc = a * acc
        u_ref[idx, :] = h
        a_ref[idx, :] = acc
        return h, acc

    def scan_body(t, carry):
        hf, af, hb, ab = carry
        hf, af = scan_step(a_f, u_f, t, hf, af)
        hb, ab = scan_step(a_b, u_b, sl - 1 - t, hb, ab)
        return hf, af, hb, ab

    hf, af, hb, ab = lax.fori_loop(0, sl, scan_body, (zeros, ones, zeros, ones), unroll=8)

    h0 = h0_ref[...]
    carry_f, carry_b = [h0[0:1]], [h0[1:2]]
    for s in range(N_SEG):
        carry_f.append(hf[s:s + 1] + af[s:s + 1] * carry_f[-1])
        sb = N_SEG - 1 - s
        carry_b.append(hb[sb:sb + 1] + ab[sb:sb + 1] * carry_b[-1])
    hlast_ref[...] = jnp.concatenate([carry_f[-1], carry_b[-1], jnp.zeros((6, LANES), F32)], axis=0)

    rc2 = min(rc, sl)
    for s in range(N_SEG):
        cf, cbk = carry_f[s], carry_b[N_SEG - 1 - s]

        def out_body(r, carry, s=s, cf=cf, cbk=cbk):
            rows = pl.ds(pl.multiple_of(s * sl + r * rc2, rc2), rc2)
            h = (u_f[rows, :] + a_f[rows, :] * cf) + (u_b[rows, :] + a_b[rows, :] * cbk)
            y_ref[rows, :] = (jax.nn.gelu(gb_ref[rows, :].astype(F32)) * h).astype(y_ref.dtype)
            return carry

        lax.fori_loop(0, sl // rc2, out_body, 0)


def _lru(p, conv_w, conv_b, w_gates, b_gates, lam, h0, dims, ts):
    rows = p.shape[0]
    w = dims.w_lru
    nct = w // LANES
    xb, gb = dims.off_lx // LANES, dims.off_lg // LANES
    nb = rows // ts
    return pl.pallas_call(
        functools.partial(_lru_kernel, rc=min(256, ts)),
        grid=(nb, nct),
        in_specs=[
            pl.BlockSpec((ts, LANES), lambda b, c: (b, xb + c)),
            pl.BlockSpec((ts, LANES), lambda b, c: (b, gb + c)),
            pl.BlockSpec((LRU_CONV, LANES), lambda b, c: (0, c)),
            pl.BlockSpec((1, LANES), lambda b, c: (0, c)),
            pl.BlockSpec((None, LANES, 4 * LANES), lambda b, c: (c, 0, 0)),
            pl.BlockSpec((None, 1, 4 * LANES), lambda b, c: (c, 0, 0)),
            pl.BlockSpec((2, LANES), lambda b, c: (0, c)),
            pl.BlockSpec((None, 8, LANES), lambda b, c: (b, 0, c)),
        ],
        out_specs=[
            pl.BlockSpec((ts, LANES), lambda b, c: (b, c)),
            pl.BlockSpec((None, 8, LANES), lambda b, c: (b, 0, c)),
        ],
        out_shape=[
            jax.ShapeDtypeStruct((rows, w), BF16),
            jax.ShapeDtypeStruct((nb, 8, w), F32),
        ],
        scratch_shapes=[pltpu.VMEM((ts + 2 * CONV_PAD, LANES), F32)] + [pltpu.VMEM((ts, LANES), F32)] * 4,
        compiler_params=_cparams(("parallel", "parallel")),
        name="lru",
    )(p, p, conv_w, conv_b, w_gates, b_gates, lam, h0)


def _final_norm_kernel(x_ref, g_ref, o_ref):
    x = x_ref[...]
    o_ref[...] = x * lax.rsqrt(jnp.mean(x * x, axis=-1, keepdims=True) + EPS) * g_ref[...]


def _final_norm(x, g, tr=512):
    rows, d = x.shape
    tr = min(tr, rows)
    return pl.pallas_call(
        _final_norm_kernel,
        grid=(rows // tr,),
        in_specs=[pl.BlockSpec((tr, d), lambda i: (i, 0)), pl.BlockSpec((1, d), lambda i: (0, 0))],
        out_specs=pl.BlockSpec((tr, d), lambda i: (i, 0)),
        out_shape=jax.ShapeDtypeStruct((rows, d), F32),
        compiler_params=_cparams(("parallel",)),
        name="final_norm",
    )(x, g.reshape(1, d))


def _rope_tables(dims):
    t = dims.seq
    rows = t // dims.grid_w
    row = jnp.repeat(jnp.arange(rows, dtype=F32), dims.grid_w)
    col = jnp.tile(jnp.arange(dims.grid_w, dtype=F32), rows)
    axis_dim = HD // 2
    inv_freq = ROPE_THETA ** (-jnp.arange(0, axis_dim, 2, dtype=F32) / axis_dim)
    ar, ac = row[:, None] * inv_freq, col[:, None] * inv_freq
    cos = jnp.concatenate([jnp.cos(ar)] * 2 + [jnp.cos(ac)] * 2, axis=-1)
    sin = jnp.concatenate([-jnp.sin(ar), jnp.sin(ar), -jnp.sin(ac), jnp.sin(ac)], axis=-1)
    return cos, sin


def _split_w_in(w_in, dims):
    nh = dims.gdn_heads
    kv = dims.kv_heads * HD
    sizes = (3 * dims.w_gdn, dims.w_gdn, nh, nh, nh, nh, dims.w_gqa, kv, kv, dims.w_lru, dims.w_lru)
    bounds = np.cumsum(sizes)[:-1].tolist()
    qkv, z, b_f, b_b, a_f, a_b, gq, gk, gv, lx, lg = jnp.split(w_in, bounds, axis=-1)
    main = jnp.concatenate([qkv, z, gq, gk, gv, lx, lg], axis=-1).astype(BF16)
    pad = jnp.zeros(w_in.shape[:2] + (LANES - 4 * nh,), w_in.dtype)
    gate = jnp.concatenate([b_f, b_b, a_f, a_b, pad], axis=-1).astype(BF16)
    return main, gate


def _gate_rows(v, dims):
    nh = dims.gdn_heads
    flat = v.reshape(v.shape[0], 1, 2 * nh).astype(F32)
    return jnp.pad(flat, ((0, 0), (0, 0), (2 * nh, LANES - 4 * nh)))


def _lru_gate_weights(gate_w, gate_b, dims):
    depth = gate_w.shape[0]
    bs = dims.w_lru // LRU_BLOCKS
    per = LANES // bs
    nct = dims.w_lru // LANES
    w = gate_w.reshape(depth, 4, nct, per, bs, bs)
    eye = jnp.eye(per, dtype=gate_w.dtype)
    dense = jnp.einsum("lgcpde,pq->lcpdgqe", w, eye).reshape(depth, nct, LANES, 4 * LANES)
    bias = gate_b.reshape(depth, 4, nct, LANES).transpose(0, 2, 1, 3).reshape(depth, nct, 1, 4 * LANES)
    return dense.astype(BF16), bias.astype(F32)


def _forward(dims, x, c, ctx, c_ctx, ada_w, ada_b, norm1_g, norm2_g, w_in, gdn_conv_w, gdn_a_log, gdn_dt_bias,
             gdn_norm_g, q_norm_g, k_norm_g, lru_conv_w, lru_conv_b, lru_gate_w, lru_gate_b, lru_lambda,
             w_out, ffn_w_up, ffn_conv_w, ffn_conv_b, ffn_w_down, final_norm_g):
    d, b, t, ct, depth = dims.d, dims.batch, dims.seq, dims.ctx, dims.depth
    f, fp = dims.ffn, dims.ffn_pad
    nh = dims.gdn_heads
    tm = min(1024, t)

    xl = x.reshape(b * t, d)
    xc = ctx.reshape(b * ct, d)

    cond8 = jnp.concatenate([c, c_ctx[None, :], jnp.zeros((8 - b - 1, d), F32)], axis=0)
    mods = _adaln(cond8, ada_w, ada_b).reshape(depth, 8, N_MOD, 1, d)
    blocks_per_seq = t // tm
    mrow_l = lambda i: i // blocks_per_seq
    mrow_c = lambda i: b

    w_main, w_gate = _split_w_in(w_in, dims)
    alog_rows, dt_rows = _gate_rows(gdn_a_log, dims), _gate_rows(gdn_dt_bias, dims)
    lru_w, lru_b = _lru_gate_weights(lru_gate_w, lru_gate_b, dims)
    wo = w_out.astype(BF16)
    wo_parts = (wo[:, :dims.w_gdn], wo[:, dims.w_gdn:dims.w_gdn + dims.w_gqa], wo[:, dims.w_gdn + dims.w_gqa:])
    padc = ((0, 0), (0, 0), (0, fp - f))
    wu_g = jnp.pad(ffn_w_up[:, :, :f], padc).astype(BF16)
    wu_u = jnp.pad(ffn_w_up[:, :, f:], padc).astype(BF16)
    cw_g, cw_u = jnp.pad(ffn_conv_w[:, :, :f], padc), jnp.pad(ffn_conv_w[:, :, f:], padc)
    cb = ffn_conv_b[:, None, :]
    cb_g, cb_u = jnp.pad(cb[:, :, :f], padc), jnp.pad(cb[:, :, f:], padc)
    wd = jnp.pad(ffn_w_down, ((0, 0), (0, fp - f), (0, 0))).astype(BF16)
    cos, sin = _rope_tables(dims)
    v_blk = dims.off_v // HD

    s_zero = jnp.zeros((b, nh, HD, HD), F32)
    h_zero = jnp.zeros((b, 8, dims.w_lru), F32)

    for l in range(depth):
        ctx_out = l < depth - 1
        p_l, g_l = _in_proj(xl, mods, l, mrow_l, norm1_g[l], w_main[l], w_gate[l], tm)
        p_c, g_c = _in_proj(xc, mods, l, mrow_c, norm1_g[l], w_main[l], w_gate[l], tm)

        qkv_l = _gdn_prep(p_l, gdn_conv_w[l], dims, t)
        qkv_c = _gdn_prep(p_c, gdn_conv_w[l], dims, ct)
        o_l, o_c = [], []
        for di, rev in enumerate((False, True)):
            oc, sc = _gdn_scan(qkv_c, g_c, alog_rows[l], dt_rows[l], s_zero, dims, ct, rev, di)
            ol, _ = _gdn_scan(qkv_l, g_l, alog_rows[l], dt_rows[l], sc, dims, t, rev, di)
            o_l.append(ol)
            o_c.append(oc)
        ya_l = _gdn_out(o_l[0], o_l[1], p_l, gdn_norm_g[l], dims)

        q_l, k_l = _qk_prep(p_l, q_norm_g[l], k_norm_g[l], cos, sin, dims, True)
        q_c, k_c = _qk_prep(p_c, q_norm_g[l], k_norm_g[l], cos, sin, dims, False)
        yb_l = _attention(q_l, [(k_l, p_l, v_blk, t), (k_c, p_c, v_blk, ct)], dims, t)

        yc_c, h_c = _lru(p_c, lru_conv_w[l], lru_conv_b[l][None, :], lru_w[l], lru_b[l], lru_lambda[l], h_zero, dims, ct)
        yc_l, _ = _lru(p_l, lru_conv_w[l], lru_conv_b[l][None, :], lru_w[l], lru_b[l], lru_lambda[l], h_c, dims, t)

        ffn_args = (wu_g[l], wu_u[l], cw_g[l], cw_u[l], cb_g[l], cb_u[l])
        xl = _proj_res((ya_l, yb_l, yc_l), [w[l] for w in wo_parts], xl, mods, l, 2, mrow_l, tm, 1024)
        a_l = _ffn_up(xl, mods, l, mrow_l, norm2_g[l], *ffn_args, t, tm)
        xl = _proj_res((a_l,), (wd[l],), xl, mods, l, 5, mrow_l, tm, 256)
        if ctx_out:
            ya_c = _gdn_out(o_c[0], o_c[1], p_c, gdn_norm_g[l], dims)
            yb_c = _attention(q_c, [(k_c, p_c, v_blk, ct)], dims, ct)
            xc = _proj_res((ya_c, yb_c, yc_c), [w[l] for w in wo_parts], xc, mods, l, 2, mrow_c, tm, 1024)
            a_c = _ffn_up(xc, mods, l, mrow_c, norm2_g[l], *ffn_args, ct, tm)
            xc = _proj_res((a_c,), (wd[l],), xc, mods, l, 5, mrow_c, tm, 256)

    return _final_norm(xl, final_norm_g).reshape(b, t, d)


def kernel(x, c, ctx, c_ctx, ada_w, ada_b, norm1_g, norm2_g, w_in, gdn_conv_w, gdn_a_log, gdn_dt_bias, gdn_norm_g, q_norm_g, k_norm_g, lru_conv_w, lru_conv_b, lru_gate_w, lru_gate_b, lru_lambda, w_out, ffn_w_up, ffn_conv_w, ffn_conv_b, ffn_w_down, final_norm_g):
    b, t, d = x.shape
    dims = Dims(d=d, batch=b, seq=t, ctx=ctx.shape[1], depth=ada_w.shape[0], grid_w=64)
    return _forward(dims, x, c, ctx, c_ctx, ada_w, ada_b, norm1_g, norm2_g, w_in, gdn_conv_w, gdn_a_log,
                    gdn_dt_bias, gdn_norm_g, q_norm_g, k_norm_g, lru_conv_w, lru_conv_b, lru_gate_w,
                    lru_gate_b, lru_lambda, w_out, ffn_w_up, ffn_conv_w, ffn_conv_b, ffn_w_down, final_norm_g)
```

```python
import functools
import math
from typing import NamedTuple

import jax
import jax.numpy as jnp
import numpy as np
from jax import lax
from jax.experimental import pallas as pl
from jax.experimental.pallas import tpu as pltpu

F32 = jnp.float32
BF16 = jnp.bfloat16
EPS = 1e-6
HD = 128
LANES = 128
N_MOD = 6
GDN_CONV = 4
LRU_CONV = 4
LRU_C = 8.0
LRU_BLOCKS = 8
ROPE_THETA = 10000.0
Q_SCALE = HD ** -0.5 * math.log2(math.e)
CH = 64
CONV_PAD = 8
VMEM_LIMIT = 56 * 1024 * 1024


class Dims(NamedTuple):
    d: int
    batch: int
    seq: int
    ctx: int
    depth: int
    grid_w: int

    @property
    def w_gdn(self):
        return self.d // 4

    @property
    def w_gqa(self):
        return self.d // 2

    @property
    def w_lru(self):
        return self.d - self.w_gdn - self.w_gqa

    @property
    def gdn_heads(self):
        return self.w_gdn // HD

    @property
    def q_heads(self):
        return self.w_gqa // HD

    @property
    def kv_heads(self):
        return self.q_heads // 4

    @property
    def ffn(self):
        return ((8 * self.d // 3 + 127) // 128) * 128

    @property
    def ffn_pad(self):
        return ((self.ffn + 511) // 512) * 512

    @property
    def p_cols(self):
        return 4 * self.w_gdn + self.w_gqa + 2 * self.kv_heads * HD + 2 * self.w_lru

    @property
    def off_z(self):
        return 3 * self.w_gdn

    @property
    def off_q(self):
        return 4 * self.w_gdn

    @property
    def off_k(self):
        return self.off_q + self.w_gqa

    @property
    def off_v(self):
        return self.off_k + self.kv_heads * HD

    @property
    def off_lx(self):
        return self.off_v + self.kv_heads * HD

    @property
    def off_lg(self):
        return self.off_lx + self.w_lru


def _cparams(sem, vmem=VMEM_LIMIT):
    return pltpu.CompilerParams(dimension_semantics=sem, vmem_limit_bytes=vmem)


def _dot(a, b):
    return jnp.dot(a, b, preferred_element_type=F32)


def _dot_exact(a, b):
    return jnp.dot(a, b, preferred_element_type=F32, precision=lax.Precision.HIGHEST)


def _dot_nt(a, b):
    return lax.dot_general(a, b, (((1,), (1,)), ((), ())), preferred_element_type=F32)


def _dot_tn(a, b):
    return lax.dot_general(a, b, (((0,), (0,)), ((), ())), preferred_element_type=F32)


def _silu(x):
    return x * jax.nn.sigmoid(x)


def _rms_mod(x, g, sh, sc):
    ms = jnp.mean(x * x, axis=-1, keepdims=True)
    return (x * lax.rsqrt(ms + EPS)) * g * (1.0 + sc) + sh


def _adaln_kernel(c_ref, w_ref, b_ref, o_ref):
    s = _silu(c_ref[...]).astype(BF16)
    o_ref[...] = _dot(s, w_ref[...].astype(BF16)) + b_ref[...]


def _adaln(cond8, ada_w, ada_b, tn=1024):
    depth, d, n = ada_w.shape
    tn = min(tn, n)
    return pl.pallas_call(
        _adaln_kernel,
        grid=(depth, n // tn),
        in_specs=[
            pl.BlockSpec((8, d), lambda l, j: (0, 0)),
            pl.BlockSpec((None, d, tn), lambda l, j: (l, 0, j)),
            pl.BlockSpec((None, 1, tn), lambda l, j: (l, 0, j)),
        ],
        out_specs=pl.BlockSpec((None, 8, tn), lambda l, j: (l, 0, j)),
        out_shape=jax.ShapeDtypeStruct((depth, 8, n), F32),
        compiler_params=_cparams(("parallel", "parallel")),
        name="adaln",
    )(cond8, ada_w, ada_b.reshape(depth, 1, n))


def _mod_spec(layer, k, mrow, d, tn=None):
    if tn is None:
        return pl.BlockSpec((None, None, None, 1, d), lambda i, j: (layer, mrow(i), k, 0, 0))
    return pl.BlockSpec((None, None, None, 1, tn), lambda i, j: (layer, mrow(i), k, 0, j))


def _in_proj_kernel(x_ref, g_ref, sh_ref, sc_ref, w_ref, wg_ref, p_ref, gate_ref, h_scr, *, rc):
    tm = x_ref.shape[0]

    @pl.when(pl.program_id(1) == 0)
    def _():
        def body(r, carry):
            rows = pl.ds(pl.multiple_of(r * rc, rc), rc)
            h = _rms_mod(x_ref[rows, :], g_ref[...], sh_ref[...], sc_ref[...]).astype(BF16)
            h_scr[rows, :] = h
            gate_ref[rows, :] = _dot(h, wg_ref[...])
            return carry

        lax.fori_loop(0, tm // rc, body, 0)

    p_ref[...] = _dot(h_scr[...], w_ref[...]).astype(p_ref.dtype)


def _in_proj(x, mods, layer, mrow, norm_g, w_main, w_gate, tm):
    rows, d = x.shape
    n = w_main.shape[1]
    tm = min(tm, rows)
    tn = next(t for t in (1536, 1024, 512, 256, LANES) if n % t == 0)
    assert rows % tm == 0
    return pl.pallas_call(
        functools.partial(_in_proj_kernel, rc=min(256, tm)),
        grid=(rows // tm, n // tn),
        in_specs=[
            pl.BlockSpec((tm, d), lambda i, j: (i, 0)),
            pl.BlockSpec((1, d), lambda i, j: (0, 0)),
            _mod_spec(layer, 0, mrow, d),
            _mod_spec(layer, 1, mrow, d),
            pl.BlockSpec((d, tn), lambda i, j: (0, j)),
            pl.BlockSpec((d, LANES), lambda i, j: (0, 0)),
        ],
        out_specs=[
            pl.BlockSpec((tm, tn), lambda i, j: (i, j)),
            pl.BlockSpec((tm, LANES), lambda i, j: (i, 0)),
        ],
        out_shape=[
            jax.ShapeDtypeStruct((rows, n), BF16),
            jax.ShapeDtypeStruct((rows, LANES), F32),
        ],
        scratch_shapes=[pltpu.VMEM((tm, d), BF16)],
        compiler_params=_cparams(("parallel", "arbitrary")),
        name="in_proj",
    )(x, norm_g.reshape(1, d), mods, mods, w_main, w_gate)


def _proj_res_kernel(*refs, n_in):
    ys, ws = refs[:n_in], refs[n_in:2 * n_in]
    x_ref, g_ref, o_ref = refs[2 * n_in:]
    acc = _dot(ys[0][...], ws[0][...])
    for y, w in zip(ys[1:], ws[1:]):
        acc = acc + _dot(y[...], w[...])
    o_ref[...] = x_ref[...] + g_ref[...] * acc


def _proj_res(ys, ws, x, mods, layer, k_gate, mrow, tm, tn):
    rows, d = x.shape
    tm = min(tm, rows)
    n_in = len(ys)
    in_specs = [pl.BlockSpec((tm, y.shape[1]), lambda i, j: (i, 0)) for y in ys]
    in_specs += [pl.BlockSpec((w.shape[0], tn), lambda i, j: (0, j)) for w in ws]
    in_specs += [pl.BlockSpec((tm, tn), lambda i, j: (i, j)), _mod_spec(layer, k_gate, mrow, d, tn)]
    return pl.pallas_call(
        functools.partial(_proj_res_kernel, n_in=n_in),
        grid=(rows // tm, d // tn),
        in_specs=in_specs,
        out_specs=pl.BlockSpec((tm, tn), lambda i, j: (i, j)),
        out_shape=jax.ShapeDtypeStruct((rows, d), F32),
        compiler_params=_cparams(("parallel", "parallel")),
        name="proj_res",
    )(*ys, *ws, x, mods)


FFN_HALO = 16


def _ffn_up_kernel(xp_ref, x_ref, xn_ref, g_ref, sh_ref, sc_ref, wg_ref, wu_ref,
                   cwg_ref, cwu_ref, cbg_ref, cbu_ref, o_ref, h_scr, *, seq_len, rc):
    tm = x_ref.shape[0]
    i = pl.program_id(0)

    @pl.when(pl.program_id(1) == 0)
    def _():
        def norm(x):
            return _rms_mod(x, g_ref[...], sh_ref[...], sc_ref[...]).astype(BF16)

        bps = max(seq_len // tm, 1)
        h_scr[0:FFN_HALO, :] = jnp.where(i % bps == 0, 0.0, norm(xp_ref[...])).astype(BF16)
        h_scr[FFN_HALO + tm:2 * FFN_HALO + tm, :] = jnp.where(i % bps == bps - 1, 0.0, norm(xn_ref[...])).astype(BF16)

        def body(r, carry):
            src = pl.ds(pl.multiple_of(r * rc, rc), rc)
            dst = pl.ds(pl.multiple_of(FFN_HALO + r * rc, FFN_HALO), rc)
            h_scr[dst, :] = norm(x_ref[src, :])
            return carry

        lax.fori_loop(0, tm // rc, body, 0)

    h = h_scr[...]
    several_seqs = seq_len < tm
    if several_seqs:
        pos = lax.rem(lax.broadcasted_iota(jnp.int32, (tm, 1), 0), seq_len)
        has_prev = pos != 0
        has_next = pos != seq_len - 1

    def conv_half(w_ref, cw_ref, cb_ref):
        u = _dot(h, w_ref[...])
        up = u[FFN_HALO - 1:FFN_HALO - 1 + tm]
        uc = u[FFN_HALO:FFN_HALO + tm]
        un = u[FFN_HALO + 1:FFN_HALO + 1 + tm]
        if several_seqs:
            up = jnp.where(has_prev, up, 0.0)
            un = jnp.where(has_next, un, 0.0)
        cw = cw_ref[...]
        return up * cw[0:1] + uc * cw[1:2] + un * cw[2:3] + cb_ref[...]

    yg = conv_half(wg_ref, cwg_ref, cbg_ref)
    yu = conv_half(wu_ref, cwu_ref, cbu_ref)
    o_ref[...] = (_silu(yg) * yu).astype(o_ref.dtype)


def _ffn_up(x, mods, layer, mrow, norm_g, w_gate, w_up, cw_g, cw_u, cb_g, cb_u, seq_len, tm, tn=512):
    rows, d = x.shape
    fp = w_gate.shape[1]
    tm = min(tm, rows)
    hb = tm // FFN_HALO
    last = rows // FFN_HALO - 1
    return pl.pallas_call(
        functools.partial(_ffn_up_kernel, seq_len=seq_len, rc=min(256, tm)),
        grid=(rows // tm, fp // tn),
        in_specs=[
            pl.BlockSpec((FFN_HALO, d), lambda i, j: (jnp.maximum(i * hb - 1, 0), 0)),
            pl.BlockSpec((tm, d), lambda i, j: (i, 0)),
            pl.BlockSpec((FFN_HALO, d), lambda i, j: (jnp.minimum((i + 1) * hb, last), 0)),
            pl.BlockSpec((1, d), lambda i, j: (0, 0)),
            _mod_spec(layer, 3, mrow, d),
            _mod_spec(layer, 4, mrow, d),
            pl.BlockSpec((d, tn), lambda i, j: (0, j)),
            pl.BlockSpec((d, tn), lambda i, j: (0, j)),
            pl.BlockSpec((3, tn), lambda i, j: (0, j)),
            pl.BlockSpec((3, tn), lambda i, j: (0, j)),
            pl.BlockSpec((1, tn), lambda i, j: (0, j)),
            pl.BlockSpec((1, tn), lambda i, j: (0, j)),
        ],
        out_specs=pl.BlockSpec((tm, tn), lambda i, j: (i, j)),
        out_shape=jax.ShapeDtypeStruct((rows, fp), BF16),
        scratch_shapes=[pltpu.VMEM((tm + 2 * FFN_HALO, d), BF16)],
        compiler_params=_cparams(("parallel", "arbitrary")),
        name="ffn_up",
    )(x, x, x, norm_g.reshape(1, d), mods, mods, w_gate, w_up, cw_g, cw_u, cb_g, cb_u)


def _fill_padded(xpad_ref, x_ref):
    ts = x_ref.shape[0]
    zeros = jnp.zeros((CONV_PAD, xpad_ref.shape[1]), F32)
    xpad_ref[0:CONV_PAD, :] = zeros
    xpad_ref[CONV_PAD + ts:2 * CONV_PAD + ts, :] = zeros
    rc = min(512, ts)

    def body(r, carry):
        src = pl.ds(pl.multiple_of(r * rc, rc), rc)
        dst = pl.ds(pl.multiple_of(CONV_PAD + r * rc, CONV_PAD), rc)
        xpad_ref[dst, :] = x_ref[src, :].astype(F32)
        return carry

    lax.fori_loop(0, ts // rc, body, 0)


def _conv4_rows(xpad_ref, start, rc, cw):
    win = xpad_ref[pl.ds(pl.multiple_of(start, CONV_PAD), rc + 2 * CONV_PAD), :]
    return (win[CONV_PAD - 2:CONV_PAD - 2 + rc] * cw[0:1] + win[CONV_PAD - 1:CONV_PAD - 1 + rc] * cw[1:2]
            + win[CONV_PAD:CONV_PAD + rc] * cw[2:3] + win[CONV_PAD + 1:CONV_PAD + 1 + rc] * cw[3:4])


def _gdn_prep_kernel(x_ref, cw_ref, o_ref, xpad, *, n_heads, rc):
    ts = x_ref.shape[0]
    j = pl.program_id(1)
    _fill_padded(xpad, x_ref)
    cw = cw_ref[...]
    is_qk = j < 2 * n_heads
    fac = jnp.where(j < n_heads, HD ** -0.5, 1.0).astype(F32)

    def body(r, carry):
        start = r * rc
        y = _silu(_conv4_rows(xpad, start, rc, cw))
        yn = y * lax.rsqrt(jnp.sum(y * y, axis=-1, keepdims=True) + EPS) * fac
        o_ref[pl.ds(pl.multiple_of(start, rc), rc), :] = jnp.where(is_qk, yn, y).astype(o_ref.dtype)
        return carry

    lax.fori_loop(0, ts // rc, body, 0)


def _gdn_prep(p, conv_w, dims, ts):
    rows = p.shape[0]
    nb = 3 * dims.gdn_heads
    return pl.pallas_call(
        functools.partial(_gdn_prep_kernel, n_heads=dims.gdn_heads, rc=min(256, ts)),
        grid=(rows // ts, nb),
        in_specs=[
            pl.BlockSpec((ts, HD), lambda b, j: (b, j)),
            pl.BlockSpec((GDN_CONV, HD), lambda b, j: (0, j)),
        ],
        out_specs=pl.BlockSpec((ts, HD), lambda b, j: (b, j)),
        out_shape=jax.ShapeDtypeStruct((rows, nb * HD), BF16),
        scratch_shapes=[pltpu.VMEM((ts + 2 * CONV_PAD, HD), F32)],
        compiler_params=_cparams(("parallel", "parallel")),
        name="gdn_prep",
    )(p, conv_w)


def _gdn_masks(n_heads, reverse):
    n = n_heads * CH
    r = np.arange(n)[:, None]
    c = np.arange(n)[None, :]
    same = (r // CH) == (c // CH)
    after = (r <= c) if reverse else (r >= c)
    incl = same & after
    strict = incl & (r != c)
    masks = [incl, strict, same, r == c]
    s = 1
    while s < CH:
        pair = (r // (2 * s)) == (c // (2 * s))
        hi, lo = (r // s) % 2 == 1, (c // s) % 2 == 0
        if reverse:
            hi, lo = (r // s) % 2 == 0, (c // s) % 2 == 1
        masks.append(pair & hi & lo & same)
        s *= 2
    return np.stack(masks).astype(np.float32)


def _chunk_cumsum(x, reverse):
    seg = x.shape[0]
    pos = lax.broadcasted_iota(jnp.int32, x.shape, 0) % CH
    k = 1
    while k < CH:
        if reverse:
            x = x + jnp.where(pos < CH - k, pltpu.roll(x, seg - k, 0), 0.0)
        else:
            x = x + jnp.where(pos >= k, pltpu.roll(x, k, 0), 0.0)
        k *= 2
    return x


def _gdn_scan_kernel(q_ref, k_ref, v_ref, g_ref, alog_ref, dt_ref, m_ref, s0_ref,
                     o_ref, sfin_ref, s_scr, g_scr, be_scr, *, n_heads, reverse, d_idx, group):
    seg = q_ref.shape[0]
    nch = seg // CH
    n = n_heads * CH
    sidx = pl.program_id(1)

    @pl.when(sidx == 0)
    def _():
        s_scr[...] = s0_ref[...]

    g_all = g_ref[...]
    log_a = -jnp.exp(alog_ref[...]) * jax.nn.softplus(g_all + dt_ref[...])
    g_cum = _chunk_cumsum(log_a, reverse)
    beta = jax.nn.sigmoid(g_all)
    for h in range(n_heads):
        cb, cg = d_idx * n_heads + h, (2 + d_idx) * n_heads + h
        g_scr[:, h * LANES:(h + 1) * LANES] = jnp.broadcast_to(g_cum[:, cg:cg + 1], (seg, LANES))
        be_scr[:, h * LANES:(h + 1) * LANES] = jnp.broadcast_to(beta[:, cb:cb + 1], (seg, LANES))

    incl, strict, eye = m_ref[0], m_ref[1], m_ref[3]
    last = 0 if reverse else CH - 1

    def stack(x):
        return jnp.concatenate([x[:, h * LANES:(h + 1) * LANES] for h in range(n_heads)], axis=0)

    def body(t, carry):
        order = [t * group + j for j in range(group)]
        rss = [pl.ds(pl.multiple_of(((nch - 1 - c) if reverse else c) * CH, CH), CH) for c in order]
        gs = [stack(g_scr[rs, :]) for rs in rss]
        bes = [stack(be_scr[rs, :]) for rs in rss]
        gls = [jnp.concatenate([jnp.broadcast_to(g[h * CH + last:h * CH + last + 1], (CH, LANES))
                                for h in range(n_heads)], axis=0) for g in gs]
        egs = [jnp.exp(g) for g in gs]
        g2s = [jnp.concatenate([g] * (n // LANES), axis=1) if n >= LANES else g[:, :n] for g in gs]
        grows = [jnp.sum(g2 * eye, axis=0, keepdims=True) for g2 in g2s]
        dms = [jnp.exp(jnp.minimum(g2 - grow, 0.0)) * incl for g2, grow in zip(g2s, grows)]
        qss = [stack(q_ref[rs, :]) for rs in rss]
        kss = [stack(k_ref[rs, :]) for rs in rss]
        ksfs = [ks.astype(F32) for ks in kss]
        kbs = [ksf * be for ksf, be in zip(ksfs, bes)]
        grams = [_dot_nt(jnp.concatenate([kb.astype(BF16), qs], axis=0), ks) for kb, qs, ks in zip(kbs, qss, kss)]
        lms = [gram[:n] * dm * strict for gram, dm in zip(grams, dms)]
        attns = [(gram[n:] * dm).astype(BF16) for gram, dm in zip(grams, dms)]
        xs = [eye - lm * m_ref[4] for lm in lms]
        lvl = 5
        s = 2
        while s < CH:
            mask = m_ref[lvl]
            xbs = [x.astype(BF16) for x in xs]
            offs = [(lm * mask).astype(BF16) for lm in lms]
            tmp = [_dot(xb, off).astype(BF16) for xb, off in zip(xbs, offs)]
            xs = [x - _dot(t1, xb) for x, t1, xb in zip(xs, tmp, xbs)]
            lvl += 1
            s *= 2
        rhss = [jnp.concatenate([stack(v_ref[rs, :]).astype(F32) * be, kb * eg], axis=1)
                for rs, be, kb, eg in zip(rss, bes, kbs, egs)]
        sols = [rhs + _dot((x - eye).astype(BF16), rhs.astype(BF16)) for x, rhs in zip(xs, rhss)]
        qds = [(qs.astype(F32) * eg).astype(BF16) for qs, eg in zip(qss, egs)]
        kds = [(ksf * jnp.exp(gl - g)).astype(BF16) for ksf, gl, g in zip(ksfs, gls, gs)]
        for j in range(group):
            rs, gl, sol, qd, kd, attn = rss[j], gls[j], sols[j], qds[j], kds[j], attns[j]
            u, wb = sol[:, :HD], sol[:, HD:].astype(BF16)
            ws_parts, qs_parts = [], []
            for h in range(n_heads):
                sh = s_scr[h].astype(BF16)
                hs = slice(h * CH, (h + 1) * CH)
                both = _dot(jnp.concatenate([wb[hs], qd[hs]], axis=0), sh)
                ws_parts.append(both[:CH])
                qs_parts.append(both[CH:])
            vnew = u - jnp.concatenate(ws_parts, axis=0)
            vnb = vnew.astype(BF16)
            o = jnp.concatenate(qs_parts, axis=0) + _dot(attn, vnb)
            for h in range(n_heads):
                hs = slice(h * CH, (h + 1) * CH)
                decay = jnp.exp(gl[h * CH:h * CH + 1, :])
                s_scr[h] = s_scr[h] * decay + _dot_tn(kd[hs], vnb[hs])
                o_ref[rs, h * HD:(h + 1) * HD] = o[hs]
        return carry

    lax.fori_loop(0, nch // group, body, 0)

    @pl.when(sidx == pl.num_programs(1) - 1)
    def _():
        sfin_ref[...] = s_scr[...]


def _gdn_scan(qkv, gates, alog_row, dt_row, s0, dims, ts, reverse, d_idx):
    rows = qkv.shape[0]
    nh = dims.gdn_heads
    w = nh * HD
    seg = min(1024, ts)
    nseg = ts // seg
    masks = jnp.asarray(_gdn_masks(nh, reverse))

    def rb(b, s):
        return b * nseg + ((nseg - 1 - s) if reverse else s)

    return pl.pallas_call(
        functools.partial(_gdn_scan_kernel, n_heads=nh, reverse=reverse, d_idx=d_idx, group=math.gcd(4, seg // CH)),
        grid=(rows // ts, nseg),
        in_specs=[
            pl.BlockSpec((seg, w), lambda b, s: (rb(b, s), 0)),
            pl.BlockSpec((seg, w), lambda b, s: (rb(b, s), 1)),
            pl.BlockSpec((seg, w), lambda b, s: (rb(b, s), 2)),
            pl.BlockSpec((seg, LANES), lambda b, s: (rb(b, s), 0)),
            pl.BlockSpec((1, LANES), lambda b, s: (0, 0)),
            pl.BlockSpec((1, LANES), lambda b, s: (0, 0)),
            pl.BlockSpec(masks.shape, lambda b, s: (0, 0, 0)),
            pl.BlockSpec((None, nh, HD, HD), lambda b, s: (b, 0, 0, 0)),
        ],
        out_specs=[
            pl.BlockSpec((seg, w), lambda b, s: (rb(b, s), 0)),
            pl.BlockSpec((None, nh, HD, HD), lambda b, s: (b, 0, 0, 0)),
        ],
        out_shape=[
            jax.ShapeDtypeStruct((rows, w), F32),
            jax.ShapeDtypeStruct(s0.shape, F32),
        ],
        scratch_shapes=[
            pltpu.VMEM((nh, HD, HD), F32),
            pltpu.VMEM((seg, w), F32),
            pltpu.VMEM((seg, w), F32),
        ],
        compiler_params=_cparams(("parallel", "arbitrary")),
        name="gdn_scan_rev" if reverse else "gdn_scan_fwd",
    )(qkv, qkv, qkv, gates, alog_row, dt_row, masks, s0)


def _gdn_out_kernel(of_ref, ob_ref, z_ref, g_ref, y_ref, *, n_heads):
    g = g_ref[...]
    for h in range(n_heads):
        cs = slice(h * HD, (h + 1) * HD)
        o = of_ref[:, cs] + ob_ref[:, cs]
        y = o * lax.rsqrt(jnp.mean(o * o, axis=-1, keepdims=True) + EPS) * g
        y_ref[:, cs] = (y * _silu(z_ref[:, cs].astype(F32))).astype(y_ref.dtype)


def _gdn_out(o_f, o_b, p, norm_g, dims, tr=512):
    rows, w = o_f.shape
    tr = min(tr, rows)
    zb = dims.off_z // w
    return pl.pallas_call(
        functools.partial(_gdn_out_kernel, n_heads=dims.gdn_heads),
        grid=(rows // tr,),
        in_specs=[
            pl.BlockSpec((tr, w), lambda i: (i, 0)),
            pl.BlockSpec((tr, w), lambda i: (i, 0)),
            pl.BlockSpec((tr, w), lambda i: (i, zb)),
            pl.BlockSpec((1, HD), lambda i: (0, 0)),
        ],
        out_specs=pl.BlockSpec((tr, w), lambda i: (i, 0)),
        out_shape=jax.ShapeDtypeStruct((rows, w), BF16),
        compiler_params=_cparams(("parallel",)),
        name="gdn_out",
    )(o_f, o_b, p, norm_g.reshape(1, HD))


def _qk_prep_kernel(q_ref, k_ref, qg_ref, kg_ref, cos_ref, sin_ref, qo_ref, ko_ref, *, rope):
    if rope:
        cos, sin = cos_ref[...], sin_ref[...]
        first_half = (lax.broadcasted_iota(jnp.int32, cos.shape, 1) % 64) < 32

    def one(x, g, scale):
        x = x.astype(F32)
        y = x * lax.rsqrt(jnp.mean(x * x, axis=-1, keepdims=True) + EPS) * g
        if rope:
            partner = jnp.where(first_half, pltpu.roll(y, 96, 1), pltpu.roll(y, 32, 1))
            y = y * cos + partner * sin
        return y * scale

    for h in range(q_ref.shape[1] // HD):
        cs = slice(h * HD, (h + 1) * HD)
        qo_ref[:, cs] = one(q_ref[:, cs], qg_ref[...], Q_SCALE).astype(qo_ref.dtype)
    for h in range(k_ref.shape[1] // HD):
        cs = slice(h * HD, (h + 1) * HD)
        ko_ref[:, cs] = one(k_ref[:, cs], kg_ref[...], 1.0).astype(ko_ref.dtype)


def _qk_prep(p, q_g, k_g, cos, sin, dims, rope, tr=512):
    rows = p.shape[0]
    wq, wk = dims.w_gqa, dims.kv_heads * HD
    tr = min(tr, rows)
    nt = cos.shape[0] // tr if rope else 1
    tab = pl.BlockSpec((tr, HD), (lambda i: (i % nt, 0)) if rope else (lambda i: (0, 0)))
    return pl.pallas_call(
        functools.partial(_qk_prep_kernel, rope=rope),
        grid=(rows // tr,),
        in_specs=[
            pl.BlockSpec((tr, wq), lambda i: (i, dims.off_q // wq)),
            pl.BlockSpec((tr, wk), lambda i: (i, dims.off_k // wk)),
            pl.BlockSpec((1, HD), lambda i: (0, 0)),
            pl.BlockSpec((1, HD), lambda i: (0, 0)),
            tab, tab,
        ],
        out_specs=[
            pl.BlockSpec((tr, wq), lambda i: (i, 0)),
            pl.BlockSpec((tr, wk), lambda i: (i, 0)),
        ],
        out_shape=[
            jax.ShapeDtypeStruct((rows, wq), BF16),
            jax.ShapeDtypeStruct((rows, wk), BF16),
        ],
        compiler_params=_cparams(("parallel",)),
        name="qk_prep",
    )(p, p, q_g.reshape(1, HD), k_g.reshape(1, HD), cos, sin)


def _attn_kernel(*refs, n_kv, tk, hpg, n_part):
    q_ref, kv, o_ref = refs[0], refs[1:1 + 2 * n_kv], refs[1 + 2 * n_kv]
    tq = q_ref.shape[0]
    hpp = hpg // n_part
    qs = [jnp.concatenate([q_ref[:, h * HD:(h + 1) * HD] for h in range(pi * hpp, (pi + 1) * hpp)], axis=0)
          for pi in range(n_part)]
    m_rows = hpp * tq

    segs = []
    for sidx in range(n_kv):
        k_ref, v_ref = kv[2 * sidx], kv[2 * sidx + 1]
        t = min(tk, k_ref.shape[0])
        segs.append((k_ref, v_ref, t, k_ref.shape[0] // t))

    def tiles(seg, it):
        k_ref, v_ref, t, _ = seg
        rs = pl.ds(pl.multiple_of(it * t, t), t)
        return k_ref[rs, :], jnp.concatenate([v_ref[rs, :], jnp.ones((t, HD), BF16)], axis=1)

    def row_max(s):
        return jnp.max(s, axis=-1, keepdims=True)

    def sweep(step, carry, first_done):
        for si, seg in enumerate(segs):
            lo = 1 if (si == 0 and first_done) else 0
            if seg[3] - lo == 1:
                carry = step(*tiles(seg, lo), carry)
            elif seg[3] - lo > 1:
                carry = lax.fori_loop(lo, seg[3], lambda it, c, seg=seg: step(*tiles(seg, it), c), carry)
        return carry

    def write(accs):
        for pi, acc in enumerate(accs):
            out = acc[:, :HD] / acc[:, HD:]
            for hh in range(hpp):
                h = pi * hpp + hh
                o_ref[:, h * HD:(h + 1) * HD] = out[hh * tq:(hh + 1) * tq].astype(o_ref.dtype)

    k0, v0 = tiles(segs[0], 0)
    ms, accs = [], []
    for q in qs:
        s = _dot_nt(q, k0)
        ms.append(row_max(s))
        accs.append(_dot(jnp.exp2(s - ms[-1]).astype(BF16), v0))

    def fast_step(k, v1, accs):
        ss = [_dot_nt(q, k) for q in qs]
        return [acc + _dot(jnp.exp2(s - m).astype(BF16), v1) for s, m, acc in zip(ss, ms, accs)]

    accs = sweep(fast_step, accs, True)
    write(accs)
    overflowed = jnp.zeros((), F32)
    for acc in accs:
        overflowed = jnp.maximum(overflowed, jnp.max(jnp.where(jnp.isfinite(acc), 0.0, 1.0)))

    @pl.when(overflowed > 0.0)
    def _():
        def exact_step(k, v1, carry):
            out = []
            for q, (m, acc) in zip(qs, carry):
                s = _dot_nt(q, k)
                m_new = jnp.maximum(m, row_max(s))
                out.append((m_new, jnp.exp2(m - m_new) * acc + _dot(jnp.exp2(s - m_new).astype(BF16), v1)))
            return out

        init = [(jnp.full((m_rows, 1), -jnp.inf, F32), jnp.zeros((m_rows, 2 * HD), F32)) for _ in qs]
        write([acc for _, acc in sweep(exact_step, init, False)])


def _attention(q, kvs, dims, tq_rows, tq=256, tk=2048, n_part=2):
    rows = q.shape[0]
    hpg = dims.q_heads // dims.kv_heads
    gw = hpg * HD
    tq = min(tq, tq_rows)
    nqb = tq_rows // tq
    in_specs = [pl.BlockSpec((tq, gw), lambda b, g, i: (b * nqb + i, g))]
    args = [q]
    for k, v_src, v_blk, s_len in kvs:
        in_specs.append(pl.BlockSpec((s_len, HD), lambda b, g, i: (b, g)))
        in_specs.append(pl.BlockSpec((s_len, HD), lambda b, g, i, v_blk=v_blk: (b, v_blk + g)))
        args += [k, v_src]
    return pl.pallas_call(
        functools.partial(_attn_kernel, n_kv=len(kvs), tk=tk, hpg=hpg, n_part=math.gcd(n_part, hpg)),
        grid=(rows // tq_rows, dims.kv_heads, nqb),
        in_specs=in_specs,
        out_specs=pl.BlockSpec((tq, gw), lambda b, g, i: (b * nqb + i, g)),
        out_shape=jax.ShapeDtypeStruct((rows, dims.w_gqa), BF16),
        compiler_params=_cparams(("parallel", "parallel", "arbitrary")),
        name="attention",
    )(*args)


N_SEG = 8


def _lru_kernel(x_ref, gb_ref, cw_ref, cb_ref, wg_ref, bg_ref, lam_ref, h0_ref,
                y_ref, hlast_ref, xpad, a_f, u_f, a_b, u_b, *, rc):
    ts = x_ref.shape[0]
    sl = ts // N_SEG
    _fill_padded(xpad, x_ref)
    cw, cb = cw_ref[...], cb_ref[...]
    wg, bg = wg_ref[...], bg_ref[...]
    sp = jax.nn.softplus(-lam_ref[...])

    def gates_body(r, carry):
        start = r * rc
        rows = pl.ds(pl.multiple_of(start, rc), rc)
        xc = _conv4_rows(xpad, start, rc, cw) + cb
        gts = jax.nn.sigmoid(_dot(xc.astype(BF16), wg) + bg)
        for d, (a_ref, u_ref) in enumerate(((a_f, u_f), (a_b, u_b))):
            rg = gts[:, (2 * d) * LANES:(2 * d + 1) * LANES]
            ig = gts[:, (2 * d + 1) * LANES:(2 * d + 2) * LANES]
            log_a = -LRU_C * rg * sp[d:d + 1]
            th = jnp.tanh(log_a)
            a_ref[rows, :] = jnp.exp(log_a)
            u_ref[rows, :] = jnp.sqrt(-2.0 * th / (1.0 - th)) * (ig * xc)
        return carry

    lax.fori_loop(0, ts // rc, gates_body, 0)

    zeros = jnp.zeros((N_SEG, LANES), F32)
    ones = jnp.ones((N_SEG, LANES), F32)

    def scan_step(a_ref, u_ref, tt, h, acc):
        idx = pl.ds(tt, N_SEG, stride=sl)
        a = a_ref[idx, :]
        h = a * h + u_ref[idx, :]
        acc = a * acc
        u_ref[idx, :] = h
        a_ref[idx, :] = acc
        return h, acc

    def scan_body(t, carry):
        hf, af, hb, ab = carry
        hf, af = scan_step(a_f, u_f, t, hf, af)
        hb, ab = scan_step(a_b, u_b, sl - 1 - t, hb, ab)
        return hf, af, hb, ab

    hf, af, hb, ab = lax.fori_loop(0, sl, scan_body, (zeros, ones, zeros, ones), unroll=8)

    h0 = h0_ref[...]
    carry_f, carry_b = [h0[0:1]], [h0[1:2]]
    for s in range(N_SEG):
        carry_f.append(hf[s:s + 1] + af[s:s + 1] * carry_f[-1])
        sb = N_SEG - 1 - s
        carry_b.append(hb[sb:sb + 1] + ab[sb:sb + 1] * carry_b[-1])
    hlast_ref[...] = jnp.concatenate([carry_f[-1], carry_b[-1], jnp.zeros((6, LANES), F32)], axis=0)

    rc2 = min(rc, sl)
    for s in range(N_SEG):
        cf, cbk = carry_f[s], carry_b[N_SEG - 1 - s]

        def out_body(r, carry, s=s, cf=cf, cbk=cbk):
            rows = pl.ds(pl.multiple_of(s * sl + r * rc2, rc2), rc2)
            h = (u_f[rows, :] + a_f[rows, :] * cf) + (u_b[rows, :] + a_b[rows, :] * cbk)
            y_ref[rows, :] = (jax.nn.gelu(gb_ref[rows, :].astype(F32)) * h).astype(y_ref.dtype)
            return carry

        lax.fori_loop(0, sl // rc2, out_body, 0)


def _lru(p, conv_w, conv_b, w_gates, b_gates, lam, h0, dims, ts):
    rows = p.shape[0]
    w = dims.w_lru
    nct = w // LANES
    xb, gb = dims.off_lx // LANES, dims.off_lg // LANES
    nb = rows // ts
    return pl.pallas_call(
        functools.partial(_lru_kernel, rc=min(256, ts)),
        grid=(nb, nct),
        in_specs=[
            pl.BlockSpec((ts, LANES), lambda b, c: (b, xb + c)),
            pl.BlockSpec((ts, LANES), lambda b, c: (b, gb + c)),
            pl.BlockSpec((LRU_CONV, LANES), lambda b, c: (0, c)),
            pl.BlockSpec((1, LANES), lambda b, c: (0, c)),
            pl.BlockSpec((None, LANES, 4 * LANES), lambda b, c: (c, 0, 0)),
            pl.BlockSpec((None, 1, 4 * LANES), lambda b, c: (c, 0, 0)),
            pl.BlockSpec((2, LANES), lambda b, c: (0, c)),
            pl.BlockSpec((None, 8, LANES), lambda b, c: (b, 0, c)),
        ],
        out_specs=[
            pl.BlockSpec((ts, LANES), lambda b, c: (b, c)),
            pl.BlockSpec((None, 8, LANES), lambda b, c: (b, 0, c)),
        ],
        out_shape=[
            jax.ShapeDtypeStruct((rows, w), BF16),
            jax.ShapeDtypeStruct((nb, 8, w), F32),
        ],
        scratch_shapes=[pltpu.VMEM((ts + 2 * CONV_PAD, LANES), F32)] + [pltpu.VMEM((ts, LANES), F32)] * 4,
        compiler_params=_cparams(("parallel", "parallel")),
        name="lru",
    )(p, p, conv_w, conv_b, w_gates, b_gates, lam, h0)


def _final_norm_kernel(x_ref, g_ref, o_ref):
    x = x_ref[...]
    o_ref[...] = x * lax.rsqrt(jnp.mean(x * x, axis=-1, keepdims=True) + EPS) * g_ref[...]


def _final_norm(x, g, tr=512):
    rows, d = x.shape
    tr = min(tr, rows)
    return pl.pallas_call(
        _final_norm_kernel,
        grid=(rows // tr,),
        in_specs=[pl.BlockSpec((tr, d), lambda i: (i, 0)), pl.BlockSpec((1, d), lambda i: (0, 0))],
        out_specs=pl.BlockSpec((tr, d), lambda i: (i, 0)),
        out_shape=jax.ShapeDtypeStruct((rows, d), F32),
        compiler_params=_cparams(("parallel",)),
        name="final_norm",
    )(x, g.reshape(1, d))


def _rope_tables(dims):
    t = dims.seq
    rows = t // dims.grid_w
    row = jnp.repeat(jnp.arange(rows, dtype=F32), dims.grid_w)
    col = jnp.tile(jnp.arange(dims.grid_w, dtype=F32), rows)
    axis_dim = HD // 2
    inv_freq = ROPE_THETA ** (-jnp.arange(0, axis_dim, 2, dtype=F32) / axis_dim)
    ar, ac = row[:, None] * inv_freq, col[:, None] * inv_freq
    cos = jnp.concatenate([jnp.cos(ar)] * 2 + [jnp.cos(ac)] * 2, axis=-1)
    sin = jnp.concatenate([-jnp.sin(ar), jnp.sin(ar), -jnp.sin(ac), jnp.sin(ac)], axis=-1)
    return cos, sin


def _split_w_in(w_in, dims):
    nh = dims.gdn_heads
    kv = dims.kv_heads * HD
    sizes = (3 * dims.w_gdn, dims.w_gdn, nh, nh, nh, nh, dims.w_gqa, kv, kv, dims.w_lru, dims.w_lru)
    bounds = np.cumsum(sizes)[:-1].tolist()
    qkv, z, b_f, b_b, a_f, a_b, gq, gk, gv, lx, lg = jnp.split(w_in, bounds, axis=-1)
    main = jnp.concatenate([qkv, z, gq, gk, gv, lx, lg], axis=-1).astype(BF16)
    pad = jnp.zeros(w_in.shape[:2] + (LANES - 4 * nh,), w_in.dtype)
    gate = jnp.concatenate([b_f, b_b, a_f, a_b, pad], axis=-1).astype(BF16)
    return main, gate


def _gate_rows(v, dims):
    nh = dims.gdn_heads
    flat = v.reshape(v.shape[0], 1, 2 * nh).astype(F32)
    return jnp.pad(flat, ((0, 0), (0, 0), (2 * nh, LANES - 4 * nh)))


def _lru_gate_weights(gate_w, gate_b, dims):
    depth = gate_w.shape[0]
    bs = dims.w_lru // LRU_BLOCKS
    per = LANES // bs
    nct = dims.w_lru // LANES
    w = gate_w.reshape(depth, 4, nct, per, bs, bs)
    eye = jnp.eye(per, dtype=gate_w.dtype)
    dense = jnp.einsum("lgcpde,pq->lcpdgqe", w, eye).reshape(depth, nct, LANES, 4 * LANES)
    bias = gate_b.reshape(depth, 4, nct, LANES).transpose(0, 2, 1, 3).reshape(depth, nct, 1, 4 * LANES)
    return dense.astype(BF16), bias.astype(F32)


def _forward(dims, x, c, ctx, c_ctx, ada_w, ada_b, norm1_g, norm2_g, w_in, gdn_conv_w, gdn_a_log, gdn_dt_bias,
             gdn_norm_g, q_norm_g, k_norm_g, lru_conv_w, lru_conv_b, lru_gate_w, lru_gate_b, lru_lambda,
             w_out, ffn_w_up, ffn_conv_w, ffn_conv_b, ffn_w_down, final_norm_g):
    d, b, t, ct, depth = dims.d, dims.batch, dims.seq, dims.ctx, dims.depth
    f, fp = dims.ffn, dims.ffn_pad
    nh = dims.gdn_heads
    tm = min(1024, t)

    xl = x.reshape(b * t, d)
    xc = ctx.reshape(b * ct, d)

    cond8 = jnp.concatenate([c, c_ctx[None, :], jnp.zeros((8 - b - 1, d), F32)], axis=0)
    mods = _adaln(cond8, ada_w, ada_b).reshape(depth, 8, N_MOD, 1, d)
    blocks_per_seq = t // tm
    mrow_l = lambda i: i // blocks_per_seq
    mrow_c = lambda i: b

    w_main, w_gate = _split_w_in(w_in, dims)
    alog_rows, dt_rows = _gate_rows(gdn_a_log, dims), _gate_rows(gdn_dt_bias, dims)
    lru_w, lru_b = _lru_gate_weights(lru_gate_w, lru_gate_b, dims)
    wo = w_out.astype(BF16)
    wo_parts = (wo[:, :dims.w_gdn], wo[:, dims.w_gdn:dims.w_gdn + dims.w_gqa], wo[:, dims.w_gdn + dims.w_gqa:])
    padc = ((0, 0), (0, 0), (0, fp - f))
    wu_g = jnp.pad(ffn_w_up[:, :, :f], padc).astype(BF16)
    wu_u = jnp.pad(ffn_w_up[:, :, f:], padc).astype(BF16)
    cw_g, cw_u = jnp.pad(ffn_conv_w[:, :, :f], padc), jnp.pad(ffn_conv_w[:, :, f:], padc)
    cb = ffn_conv_b[:, None, :]
    cb_g, cb_u = jnp.pad(cb[:, :, :f], padc), jnp.pad(cb[:, :, f:], padc)
    wd = jnp.pad(ffn_w_down, ((0, 0), (0, fp - f), (0, 0))).astype(BF16)
    cos, sin = _rope_tables(dims)
    v_blk = dims.off_v // HD

    s_zero = jnp.zeros((b, nh, HD, HD), F32)
    h_zero = jnp.zeros((b, 8, dims.w_lru), F32)

    for l in range(depth):
        ctx_out = l < depth - 1
        p_l, g_l = _in_proj(xl, mods, l, mrow_l, norm1_g[l], w_main[l], w_gate[l], tm)
        p_c, g_c = _in_proj(xc, mods, l, mrow_c, norm1_g[l], w_main[l], w_gate[l], tm)

        qkv_l = _gdn_prep(p_l, gdn_conv_w[l], dims, t)
        qkv_c = _gdn_prep(p_c, gdn_conv_w[l], dims, ct)
        o_l, o_c = [], []
        for di, rev in enumerate((False, True)):
            oc, sc = _gdn_scan(qkv_c, g_c, alog_rows[l], dt_rows[l], s_zero, dims, ct, rev, di)
            ol, _ = _gdn_scan(qkv_l, g_l, alog_rows[l], dt_rows[l], sc, dims, t, rev, di)
            o_l.append(ol)
            o_c.append(oc)
        ya_l = _gdn_out(o_l[0], o_l[1], p_l, gdn_norm_g[l], dims)

        q_l, k_l = _qk_prep(p_l, q_norm_g[l], k_norm_g[l], cos, sin, dims, True)
        q_c, k_c = _qk_prep(p_c, q_norm_g[l], k_norm_g[l], cos, sin, dims, False)
        yb_l = _attention(q_l, [(k_l, p_l, v_blk, t), (k_c, p_c, v_blk, ct)], dims, t)

        yc_c, h_c = _lru(p_c, lru_conv_w[l], lru_conv_b[l][None, :], lru_w[l], lru_b[l], lru_lambda[l], h_zero, dims, ct)
        yc_l, _ = _lru(p_l, lru_conv_w[l], lru_conv_b[l][None, :], lru_w[l], lru_b[l], lru_lambda[l], h_c, dims, t)

        ffn_args = (wu_g[l], wu_u[l], cw_g[l], cw_u[l], cb_g[l], cb_u[l])
        xl = _proj_res((ya_l, yb_l, yc_l), [w[l] for w in wo_parts], xl, mods, l, 2, mrow_l, tm, 1024)
        a_l = _ffn_up(xl, mods, l, mrow_l, norm2_g[l], *ffn_args, t, tm)
        xl = _proj_res((a_l,), (wd[l],), xl, mods, l, 5, mrow_l, tm, 512)
        if ctx_out:
            ya_c = _gdn_out(o_c[0], o_c[1], p_c, gdn_norm_g[l], dims)
            yb_c = _attention(q_c, [(k_c, p_c, v_blk, ct)], dims, ct)
            xc = _proj_res((ya_c, yb_c, yc_c), [w[l] for w in wo_parts], xc, mods, l, 2, mrow_c, tm, 1024)
            a_c = _ffn_up(xc, mods, l, mrow_c, norm2_g[l], *ffn_args, ct, tm)
            xc = _proj_res((a_c,), (wd[l],), xc, mods, l, 5, mrow_c, tm, 512)

    return _final_norm(xl, final_norm_g).reshape(b, t, d)


def kernel(x, c, ctx, c_ctx, ada_w, ada_b, norm1_g, norm2_g, w_in, gdn_conv_w, gdn_a_log, gdn_dt_bias, gdn_norm_g, q_norm_g, k_norm_g, lru_conv_w, lru_conv_b, lru_gate_w, lru_gate_b, lru_lambda, w_out, ffn_w_up, ffn_conv_w, ffn_conv_b, ffn_w_down, final_norm_g):
    b, t, d = x.shape
    dims = Dims(d=d, batch=b, seq=t, ctx=ctx.shape[1], depth=ada_w.shape[0], grid_w=64)
    return _forward(dims, x, c, ctx, c_ctx, ada_w, ada_b, norm1_g, norm2_g, w_in, gdn_conv_w, gdn_a_log,
                    gdn_dt_bias, gdn_norm_g, q_norm_g, k_norm_g, lru_conv_w, lru_conv_b, lru_gate_w,
                    lru_gate_b, lru_lambda, w_out, ffn_w_up, ffn_conv_w, ffn_conv_b, ffn_w_down, final_norm_g)
```

```python
import functools
import math
from typing import NamedTuple

import jax
import jax.numpy as jnp
import numpy as np
from jax import lax
from jax.experimental import pallas as pl
from jax.experimental.pallas import tpu as pltpu

F32 = jnp.float32
BF16 = jnp.bfloat16
EPS = 1e-6
HD = 128
LANES = 128
N_MOD = 6
GDN_CONV = 4
LRU_CONV = 4
LRU_C = 8.0
LRU_BLOCKS = 8
ROPE_THETA = 10000.0
Q_SCALE = HD ** -0.5 * math.log2(math.e)
CH = 64
CONV_PAD = 8
VMEM_LIMIT = 56 * 1024 * 1024


class Dims(NamedTuple):
    d: int
    batch: int
    seq: int
    ctx: int
    depth: int
    grid_w: int

    @property
    def w_gdn(self):
        return self.d // 4

    @property
    def w_gqa(self):
        return self.d // 2

    @property
    def w_lru(self):
        return self.d - self.w_gdn - self.w_gqa

    @property
    def gdn_heads(self):
        return self.w_gdn // HD

    @property
    def q_heads(self):
        return self.w_gqa // HD

    @property
    def kv_heads(self):
        return self.q_heads // 4

    @property
    def ffn(self):
        return ((8 * self.d // 3 + 127) // 128) * 128

    @property
    def ffn_pad(self):
        return ((self.ffn + 511) // 512) * 512

    @property
    def p_cols(self):
        return 4 * self.w_gdn + self.w_gqa + 2 * self.kv_heads * HD + 2 * self.w_lru

    @property
    def off_z(self):
        return 3 * self.w_gdn

    @property
    def off_q(self):
        return 4 * self.w_gdn

    @property
    def off_k(self):
        return self.off_q + self.w_gqa

    @property
    def off_v(self):
        return self.off_k + self.kv_heads * HD

    @property
    def off_lx(self):
        return self.off_v + self.kv_heads * HD

    @property
    def off_lg(self):
        return self.off_lx + self.w_lru


def _cparams(sem, vmem=VMEM_LIMIT):
    return pltpu.CompilerParams(dimension_semantics=sem, vmem_limit_bytes=vmem)


def _dot(a, b):
    return jnp.dot(a, b, preferred_element_type=F32)


def _dot_exact(a, b):
    return jnp.dot(a, b, preferred_element_type=F32, precision=lax.Precision.HIGHEST)


def _dot_nt(a, b):
    return lax.dot_general(a, b, (((1,), (1,)), ((), ())), preferred_element_type=F32)


def _dot_tn(a, b):
    return lax.dot_general(a, b, (((0,), (0,)), ((), ())), preferred_element_type=F32)


def _sigmoid(x):
    return 0.5 * jnp.tanh(0.5 * x) + 0.5


def _silu(x):
    return x * _sigmoid(x)


def _rms_mod(x, g, sh, sc):
    ms = jnp.mean(x * x, axis=-1, keepdims=True)
    return (x * lax.rsqrt(ms + EPS)) * g * (1.0 + sc) + sh


def _adaln_kernel(c_ref, w_ref, b_ref, o_ref):
    s = _silu(c_ref[...]).astype(BF16)
    o_ref[...] = _dot(s, w_ref[...].astype(BF16)) + b_ref[...]


def _adaln(cond8, ada_w, ada_b, tn=1024):
    depth, d, n = ada_w.shape
    tn = min(tn, n)
    return pl.pallas_call(
        _adaln_kernel,
        grid=(depth, n // tn),
        in_specs=[
            pl.BlockSpec((8, d), lambda l, j: (0, 0)),
            pl.BlockSpec((None, d, tn), lambda l, j: (l, 0, j)),
            pl.BlockSpec((None, 1, tn), lambda l, j: (l, 0, j)),
        ],
        out_specs=pl.BlockSpec((None, 8, tn), lambda l, j: (l, 0, j)),
        out_shape=jax.ShapeDtypeStruct((depth, 8, n), F32),
        compiler_params=_cparams(("parallel", "parallel")),
        name="adaln",
    )(cond8, ada_w, ada_b.reshape(depth, 1, n))


def _mod_spec(layer, k, mrow, d, tn=None):
    if tn is None:
        return pl.BlockSpec((None, None, None, 1, d), lambda i, j: (layer, mrow(i), k, 0, 0))
    return pl.BlockSpec((None, None, None, 1, tn), lambda i, j: (layer, mrow(i), k, 0, j))


def _in_proj_kernel(x_ref, g_ref, sh_ref, sc_ref, w_ref, wg_ref, p_ref, gate_ref, h_scr, *, rc):
    tm = x_ref.shape[0]

    @pl.when(pl.program_id(1) == 0)
    def _():
        def body(r, carry):
            rows = pl.ds(pl.multiple_of(r * rc, rc), rc)
            h = _rms_mod(x_ref[rows, :], g_ref[...], sh_ref[...], sc_ref[...]).astype(BF16)
            h_scr[rows, :] = h
            gate_ref[rows, :] = _dot(h, wg_ref[...])
            return carry

        lax.fori_loop(0, tm // rc, body, 0)

    p_ref[...] = _dot(h_scr[...], w_ref[...]).astype(p_ref.dtype)


def _in_proj(x, mods, layer, mrow, norm_g, w_main, w_gate, tm):
    rows, d = x.shape
    n = w_main.shape[-1]
    tm = min(tm, rows)
    tn = next(t for t in (1536, 1024, 512, 256, LANES) if n % t == 0)
    assert rows % tm == 0
    return pl.pallas_call(
        functools.partial(_in_proj_kernel, rc=min(256, tm)),
        grid=(rows // tm, n // tn),
        in_specs=[
            pl.BlockSpec((tm, d), lambda i, j: (i, 0)),
            pl.BlockSpec((1, d), lambda i, j: (0, 0)),
            _mod_spec(layer, 0, mrow, d),
            _mod_spec(layer, 1, mrow, d),
            pl.BlockSpec((None, d, tn), lambda i, j: (layer, 0, j)),
            pl.BlockSpec((None, d, LANES), lambda i, j: (layer, 0, 0)),
        ],
        out_specs=[
            pl.BlockSpec((tm, tn), lambda i, j: (i, j)),
            pl.BlockSpec((tm, LANES), lambda i, j: (i, 0)),
        ],
        out_shape=[
            jax.ShapeDtypeStruct((rows, n), BF16),
            jax.ShapeDtypeStruct((rows, LANES), F32),
        ],
        scratch_shapes=[pltpu.VMEM((tm, d), BF16)],
        compiler_params=_cparams(("parallel", "arbitrary")),
        name="in_proj",
    )(x, norm_g.reshape(1, d), mods, mods, w_main, w_gate)


def _proj_res_kernel(*refs, n_in):
    ys, ws = refs[:n_in], refs[n_in:2 * n_in]
    x_ref, g_ref, o_ref = refs[2 * n_in:]
    acc = _dot(ys[0][...], ws[0][...])
    for y, w in zip(ys[1:], ws[1:]):
        acc = acc + _dot(y[...], w[...])
    o_ref[...] = x_ref[...] + g_ref[...] * acc


def _proj_res(ys, ws, x, mods, layer, k_gate, mrow, tm, tn):
    rows, d = x.shape
    tm = min(tm, rows)
    n_in = len(ys)
    in_specs = [pl.BlockSpec((tm, y.shape[1]), lambda i, j: (i, 0)) for y in ys]
    in_specs += [pl.BlockSpec((None, w.shape[1], tn), lambda i, j: (layer, 0, j)) for w in ws]
    in_specs += [pl.BlockSpec((tm, tn), lambda i, j: (i, j)), _mod_spec(layer, k_gate, mrow, d, tn)]
    return pl.pallas_call(
        functools.partial(_proj_res_kernel, n_in=n_in),
        grid=(rows // tm, d // tn),
        in_specs=in_specs,
        out_specs=pl.BlockSpec((tm, tn), lambda i, j: (i, j)),
        out_shape=jax.ShapeDtypeStruct((rows, d), F32),
        compiler_params=_cparams(("parallel", "parallel")),
        name="proj_res",
    )(*ys, *ws, x, mods)


FFN_HALO = 16


def _ffn_up_kernel(xp_ref, x_ref, xn_ref, g_ref, sh_ref, sc_ref, wg_ref, wu_ref,
                   cwg_ref, cwu_ref, cbg_ref, cbu_ref, o_ref, h_scr, *, seq_len, rc):
    tm = x_ref.shape[0]
    i = pl.program_id(0)

    @pl.when(pl.program_id(1) == 0)
    def _():
        def norm(x):
            return _rms_mod(x, g_ref[...], sh_ref[...], sc_ref[...]).astype(BF16)

        bps = max(seq_len // tm, 1)
        h_scr[0:FFN_HALO, :] = jnp.where(i % bps == 0, 0.0, norm(xp_ref[...])).astype(BF16)
        h_scr[FFN_HALO + tm:2 * FFN_HALO + tm, :] = jnp.where(i % bps == bps - 1, 0.0, norm(xn_ref[...])).astype(BF16)

        def body(r, carry):
            src = pl.ds(pl.multiple_of(r * rc, rc), rc)
            dst = pl.ds(pl.multiple_of(FFN_HALO + r * rc, FFN_HALO), rc)
            h_scr[dst, :] = norm(x_ref[src, :])
            return carry

        lax.fori_loop(0, tm // rc, body, 0)

    h = h_scr[...]
    several_seqs = seq_len < tm
    if several_seqs:
        pos = lax.rem(lax.broadcasted_iota(jnp.int32, (tm, 1), 0), seq_len)
        has_prev = pos != 0
        has_next = pos != seq_len - 1

    def conv_half(w_ref, cw_ref, cb_ref):
        u = _dot(h, w_ref[...])
        up = pltpu.roll(u, 1, 0)[FFN_HALO:FFN_HALO + tm]
        uc = u[FFN_HALO:FFN_HALO + tm]
        un = pltpu.roll(u, tm + 2 * FFN_HALO - 1, 0)[FFN_HALO:FFN_HALO + tm]
        if several_seqs:
            up = jnp.where(has_prev, up, 0.0)
            un = jnp.where(has_next, un, 0.0)
        cw = cw_ref[...]
        return up * cw[0:1] + uc * cw[1:2] + un * cw[2:3] + cb_ref[...]

    yg = conv_half(wg_ref, cwg_ref, cbg_ref)
    yu = conv_half(wu_ref, cwu_ref, cbu_ref)
    o_ref[...] = (_silu(yg) * yu).astype(o_ref.dtype)


def _ffn_up(x, mods, layer, mrow, norm_g, w_gate, w_up, cw_g, cw_u, cb_g, cb_u, seq_len, tm, tn=512):
    rows, d = x.shape
    fp = w_gate.shape[-1]
    tm = min(tm, rows)
    hb = tm // FFN_HALO
    last = rows // FFN_HALO - 1
    return pl.pallas_call(
        functools.partial(_ffn_up_kernel, seq_len=seq_len, rc=min(256, tm)),
        grid=(rows // tm, fp // tn),
        in_specs=[
            pl.BlockSpec((FFN_HALO, d), lambda i, j: (jnp.maximum(i * hb - 1, 0), 0)),
            pl.BlockSpec((tm, d), lambda i, j: (i, 0)),
            pl.BlockSpec((FFN_HALO, d), lambda i, j: (jnp.minimum((i + 1) * hb, last), 0)),
            pl.BlockSpec((1, d), lambda i, j: (0, 0)),
            _mod_spec(layer, 3, mrow, d),
            _mod_spec(layer, 4, mrow, d),
            pl.BlockSpec((None, d, tn), lambda i, j: (layer, 0, j)),
            pl.BlockSpec((None, d, tn), lambda i, j: (layer, 0, j)),
            pl.BlockSpec((None, 3, tn), lambda i, j: (layer, 0, j)),
            pl.BlockSpec((None, 3, tn), lambda i, j: (layer, 0, j)),
            pl.BlockSpec((None, 1, tn), lambda i, j: (layer, 0, j)),
            pl.BlockSpec((None, 1, tn), lambda i, j: (layer, 0, j)),
        ],
        out_specs=pl.BlockSpec((tm, tn), lambda i, j: (i, j)),
        out_shape=jax.ShapeDtypeStruct((rows, fp), BF16),
        scratch_shapes=[pltpu.VMEM((tm + 2 * FFN_HALO, d), BF16)],
        compiler_params=_cparams(("parallel", "arbitrary")),
        name="ffn_up",
    )(x, x, x, norm_g.reshape(1, d), mods, mods, w_gate, w_up, cw_g, cw_u, cb_g, cb_u)


def _fill_padded(xpad_ref, x_ref):
    ts = x_ref.shape[0]
    zeros = jnp.zeros((CONV_PAD, xpad_ref.shape[1]), F32)
    xpad_ref[0:CONV_PAD, :] = zeros
    xpad_ref[CONV_PAD + ts:2 * CONV_PAD + ts, :] = zeros
    rc = min(512, ts)

    def body(r, carry):
        src = pl.ds(pl.multiple_of(r * rc, rc), rc)
        dst = pl.ds(pl.multiple_of(CONV_PAD + r * rc, CONV_PAD), rc)
        xpad_ref[dst, :] = x_ref[src, :].astype(F32)
        return carry

    lax.fori_loop(0, ts // rc, body, 0)


def _conv4_rows(xpad_ref, start, rc, cw):
    win = xpad_ref[pl.ds(pl.multiple_of(start, CONV_PAD), rc + 2 * CONV_PAD), :]
    return (win[CONV_PAD - 2:CONV_PAD - 2 + rc] * cw[0:1] + win[CONV_PAD - 1:CONV_PAD - 1 + rc] * cw[1:2]
            + win[CONV_PAD:CONV_PAD + rc] * cw[2:3] + win[CONV_PAD + 1:CONV_PAD + 1 + rc] * cw[3:4])


def _gdn_prep_kernel(x_ref, cw_ref, o_ref, xpad, *, n_heads, rc):
    ts = x_ref.shape[0]
    j = pl.program_id(1)
    _fill_padded(xpad, x_ref)
    cw = cw_ref[...]
    is_qk = j < 2 * n_heads
    fac = jnp.where(j < n_heads, HD ** -0.5, 1.0).astype(F32)

    def body(r, carry):
        start = r * rc
        y = _silu(_conv4_rows(xpad, start, rc, cw))
        yn = y * lax.rsqrt(jnp.sum(y * y, axis=-1, keepdims=True) + EPS) * fac
        o_ref[pl.ds(pl.multiple_of(start, rc), rc), :] = jnp.where(is_qk, yn, y).astype(o_ref.dtype)
        return carry

    lax.fori_loop(0, ts // rc, body, 0)


def _gdn_prep(p, conv_w, dims, ts):
    rows = p.shape[0]
    nb = 3 * dims.gdn_heads
    return pl.pallas_call(
        functools.partial(_gdn_prep_kernel, n_heads=dims.gdn_heads, rc=min(256, ts)),
        grid=(rows // ts, nb),
        in_specs=[
            pl.BlockSpec((ts, HD), lambda b, j: (b, j)),
            pl.BlockSpec((GDN_CONV, HD), lambda b, j: (0, j)),
        ],
        out_specs=pl.BlockSpec((ts, HD), lambda b, j: (b, j)),
        out_shape=jax.ShapeDtypeStruct((rows, nb * HD), BF16),
        scratch_shapes=[pltpu.VMEM((ts + 2 * CONV_PAD, HD), F32)],
        compiler_params=_cparams(("parallel", "parallel")),
        name="gdn_prep",
    )(p, conv_w)


def _gdn_masks(n_heads, reverse):
    n = n_heads * CH
    r = np.arange(n)[:, None]
    c = np.arange(n)[None, :]
    same = (r // CH) == (c // CH)
    after = (r <= c) if reverse else (r >= c)
    incl = same & after
    strict = incl & (r != c)
    masks = [incl, strict, same, r == c]
    s = 1
    while s < CH:
        pair = (r // (2 * s)) == (c // (2 * s))
        hi, lo = (r // s) % 2 == 1, (c // s) % 2 == 0
        if reverse:
            hi, lo = (r // s) % 2 == 0, (c // s) % 2 == 1
        masks.append(pair & hi & lo & same)
        s *= 2
    return np.stack(masks).astype(np.float32)


def _chunk_cumsum(x, reverse):
    seg = x.shape[0]
    pos = lax.broadcasted_iota(jnp.int32, x.shape, 0) % CH
    k = 1
    while k < CH:
        if reverse:
            x = x + jnp.where(pos < CH - k, pltpu.roll(x, seg - k, 0), 0.0)
        else:
            x = x + jnp.where(pos >= k, pltpu.roll(x, k, 0), 0.0)
        k *= 2
    return x


def _gdn_scan_kernel(q_ref, k_ref, v_ref, g_ref, alog_ref, dt_ref, m_ref, s0_ref,
                     o_ref, sfin_ref, s_scr, g_scr, be_scr, *, n_heads, reverse, d_idx, group):
    seg = q_ref.shape[0]
    nch = seg // CH
    n = n_heads * CH
    sidx = pl.program_id(1)

    @pl.when(sidx == 0)
    def _():
        s_scr[...] = s0_ref[...]

    g_all = g_ref[...]
    log_a = -jnp.exp(alog_ref[...]) * jax.nn.softplus(g_all + dt_ref[...])
    g_cum = _chunk_cumsum(log_a, reverse)
    beta = _sigmoid(g_all)
    for h in range(n_heads):
        cb, cg = d_idx * n_heads + h, (2 + d_idx) * n_heads + h
        g_scr[:, h * LANES:(h + 1) * LANES] = jnp.broadcast_to(g_cum[:, cg:cg + 1], (seg, LANES))
        be_scr[:, h * LANES:(h + 1) * LANES] = jnp.broadcast_to(beta[:, cb:cb + 1], (seg, LANES))

    incl, strict, eye = m_ref[0], m_ref[1], m_ref[3]
    last = 0 if reverse else CH - 1

    def stack(x):
        return jnp.concatenate([x[:, h * LANES:(h + 1) * LANES] for h in range(n_heads)], axis=0)

    def body(t, carry):
        order = [t * group + j for j in range(group)]
        rss = [pl.ds(pl.multiple_of(((nch - 1 - c) if reverse else c) * CH, CH), CH) for c in order]
        gs = [stack(g_scr[rs, :]) for rs in rss]
        bes = [stack(be_scr[rs, :]) for rs in rss]
        gls = [jnp.concatenate([jnp.broadcast_to(g[h * CH + last:h * CH + last + 1], (CH, LANES))
                                for h in range(n_heads)], axis=0) for g in gs]
        egs = [jnp.exp(g) for g in gs]
        g2s = [jnp.concatenate([g] * (n // LANES), axis=1) if n >= LANES else g[:, :n] for g in gs]
        grows = [jnp.sum(g2 * eye, axis=0, keepdims=True) for g2 in g2s]
        dms = [jnp.exp(jnp.minimum(g2 - grow, 0.0)) * incl for g2, grow in zip(g2s, grows)]
        qss = [stack(q_ref[rs, :]) for rs in rss]
        kss = [stack(k_ref[rs, :]) for rs in rss]
        ksfs = [ks.astype(F32) for ks in kss]
        kbs = [ksf * be for ksf, be in zip(ksfs, bes)]
        grams = [_dot_nt(jnp.concatenate([kb.astype(BF16), qs], axis=0), ks) for kb, qs, ks in zip(kbs, qss, kss)]
        lms = [gram[:n] * dm * strict for gram, dm in zip(grams, dms)]
        attns = [(gram[n:] * dm).astype(BF16) for gram, dm in zip(grams, dms)]
        xs = [eye - lm * m_ref[4] for lm in lms]
        lvl = 5
        s = 2
        while s < CH:
            mask = m_ref[lvl]
            xbs = [x.astype(BF16) for x in xs]
            offs = [(lm * mask).astype(BF16) for lm in lms]
            tmp = [_dot(xb, off).astype(BF16) for xb, off in zip(xbs, offs)]
            xs = [x - _dot(t1, xb) for x, t1, xb in zip(xs, tmp, xbs)]
            lvl += 1
            s *= 2
        rhss = [jnp.concatenate([stack(v_ref[rs, :]).astype(F32) * be, kb * eg], axis=1)
                for rs, be, kb, eg in zip(rss, bes, kbs, egs)]
        sols = [rhs + _dot((x - eye).astype(BF16), rhs.astype(BF16)) for x, rhs in zip(xs, rhss)]
        qds = [(qs.astype(F32) * eg).astype(BF16) for qs, eg in zip(qss, egs)]
        kds = [(ksf * jnp.exp(gl - g)).astype(BF16) for ksf, gl, g in zip(ksfs, gls, gs)]
        for j in range(group):
            rs, gl, sol, qd, kd, attn = rss[j], gls[j], sols[j], qds[j], kds[j], attns[j]
            u, wb = sol[:, :HD], sol[:, HD:].astype(BF16)
            ws_parts, qs_parts = [], []
            for h in range(n_heads):
                sh = s_scr[h].astype(BF16)
                hs = slice(h * CH, (h + 1) * CH)
                both = _dot(jnp.concatenate([wb[hs], qd[hs]], axis=0), sh)
                ws_parts.append(both[:CH])
                qs_parts.append(both[CH:])
            vnew = u - jnp.concatenate(ws_parts, axis=0)
            vnb = vnew.astype(BF16)
            o = jnp.concatenate(qs_parts, axis=0) + _dot(attn, vnb)
            for h in range(n_heads):
                hs = slice(h * CH, (h + 1) * CH)
                decay = jnp.exp(gl[h * CH:h * CH + 1, :])
                s_scr[h] = s_scr[h] * decay + _dot_tn(kd[hs], vnb[hs])
                o_ref[rs, h * HD:(h + 1) * HD] = o[hs]
        return carry

    lax.fori_loop(0, nch // group, body, 0)

    @pl.when(sidx == pl.num_programs(1) - 1)
    def _():
        sfin_ref[...] = s_scr[...]


def _gdn_scan(qkv, gates, alog_row, dt_row, s0, dims, ts, reverse, d_idx):
    rows = qkv.shape[0]
    nh = dims.gdn_heads
    w = nh * HD
    seg = min(1024, ts)
    nseg = ts // seg
    masks = jnp.asarray(_gdn_masks(nh, reverse))

    def rb(b, s):
        return b * nseg + ((nseg - 1 - s) if reverse else s)

    return pl.pallas_call(
        functools.partial(_gdn_scan_kernel, n_heads=nh, reverse=reverse, d_idx=d_idx, group=math.gcd(4, seg // CH)),
        grid=(rows // ts, nseg),
        in_specs=[
            pl.BlockSpec((seg, w), lambda b, s: (rb(b, s), 0)),
            pl.BlockSpec((seg, w), lambda b, s: (rb(b, s), 1)),
            pl.BlockSpec((seg, w), lambda b, s: (rb(b, s), 2)),
            pl.BlockSpec((seg, LANES), lambda b, s: (rb(b, s), 0)),
            pl.BlockSpec((1, LANES), lambda b, s: (0, 0)),
            pl.BlockSpec((1, LANES), lambda b, s: (0, 0)),
            pl.BlockSpec(masks.shape, lambda b, s: (0, 0, 0)),
            pl.BlockSpec((None, nh, HD, HD), lambda b, s: (b, 0, 0, 0)),
        ],
        out_specs=[
            pl.BlockSpec((seg, w), lambda b, s: (rb(b, s), 0)),
            pl.BlockSpec((None, nh, HD, HD), lambda b, s: (b, 0, 0, 0)),
        ],
        out_shape=[
            jax.ShapeDtypeStruct((rows, w), F32),
            jax.ShapeDtypeStruct(s0.shape, F32),
        ],
        scratch_shapes=[
            pltpu.VMEM((nh, HD, HD), F32),
            pltpu.VMEM((seg, w), F32),
            pltpu.VMEM((seg, w), F32),
        ],
        compiler_params=_cparams(("parallel", "arbitrary")),
        name="gdn_scan_rev" if reverse else "gdn_scan_fwd",
    )(qkv, qkv, qkv, gates, alog_row, dt_row, masks, s0)


def _gdn_out_kernel(of_ref, ob_ref, z_ref, g_ref, y_ref, *, n_heads):
    g = g_ref[...]
    for h in range(n_heads):
        cs = slice(h * HD, (h + 1) * HD)
        o = of_ref[:, cs] + ob_ref[:, cs]
        y = o * lax.rsqrt(jnp.mean(o * o, axis=-1, keepdims=True) + EPS) * g
        y_ref[:, cs] = (y * _silu(z_ref[:, cs].astype(F32))).astype(y_ref.dtype)


def _gdn_out(o_f, o_b, p, norm_g, dims, tr=512):
    rows, w = o_f.shape
    tr = min(tr, rows)
    zb = dims.off_z // w
    return pl.pallas_call(
        functools.partial(_gdn_out_kernel, n_heads=dims.gdn_heads),
        grid=(rows // tr,),
        in_specs=[
            pl.BlockSpec((tr, w), lambda i: (i, 0)),
            pl.BlockSpec((tr, w), lambda i: (i, 0)),
            pl.BlockSpec((tr, w), lambda i: (i, zb)),
            pl.BlockSpec((1, HD), lambda i: (0, 0)),
        ],
        out_specs=pl.BlockSpec((tr, w), lambda i: (i, 0)),
        out_shape=jax.ShapeDtypeStruct((rows, w), BF16),
        compiler_params=_cparams(("parallel",)),
        name="gdn_out",
    )(o_f, o_b, p, norm_g.reshape(1, HD))


def _qk_prep_kernel(q_ref, k_ref, qg_ref, kg_ref, cos_ref, sin_ref, qo_ref, ko_ref, *, rope):
    if rope:
        cos, sin = cos_ref[...], sin_ref[...]
        first_half = (lax.broadcasted_iota(jnp.int32, cos.shape, 1) % 64) < 32

    def one(x, g, scale):
        x = x.astype(F32)
        y = x * lax.rsqrt(jnp.mean(x * x, axis=-1, keepdims=True) + EPS) * g
        if rope:
            partner = jnp.where(first_half, pltpu.roll(y, 96, 1), pltpu.roll(y, 32, 1))
            y = y * cos + partner * sin
        return y * scale

    for h in range(q_ref.shape[1] // HD):
        cs = slice(h * HD, (h + 1) * HD)
        qo_ref[:, cs] = one(q_ref[:, cs], qg_ref[...], Q_SCALE).astype(qo_ref.dtype)
    for h in range(k_ref.shape[1] // HD):
        cs = slice(h * HD, (h + 1) * HD)
        ko_ref[:, cs] = one(k_ref[:, cs], kg_ref[...], 1.0).astype(ko_ref.dtype)


def _qk_prep(p, q_g, k_g, cos, sin, dims, rope, tr=512):
    rows = p.shape[0]
    wq, wk = dims.w_gqa, dims.kv_heads * HD
    tr = min(tr, rows)
    nt = cos.shape[0] // tr if rope else 1
    tab = pl.BlockSpec((tr, HD), (lambda i: (i % nt, 0)) if rope else (lambda i: (0, 0)))
    return pl.pallas_call(
        functools.partial(_qk_prep_kernel, rope=rope),
        grid=(rows // tr,),
        in_specs=[
            pl.BlockSpec((tr, wq), lambda i: (i, dims.off_q // wq)),
            pl.BlockSpec((tr, wk), lambda i: (i, dims.off_k // wk)),
            pl.BlockSpec((1, HD), lambda i: (0, 0)),
            pl.BlockSpec((1, HD), lambda i: (0, 0)),
            tab, tab,
        ],
        out_specs=[
            pl.BlockSpec((tr, wq), lambda i: (i, 0)),
            pl.BlockSpec((tr, wk), lambda i: (i, 0)),
        ],
        out_shape=[
            jax.ShapeDtypeStruct((rows, wq), BF16),
            jax.ShapeDtypeStruct((rows, wk), BF16),
        ],
        compiler_params=_cparams(("parallel",)),
        name="qk_prep",
    )(p, p, q_g.reshape(1, HD), k_g.reshape(1, HD), cos, sin)


def _attn_kernel(*refs, n_kv, tk, hpg, n_part):
    q_ref, kv, o_ref = refs[0], refs[1:1 + 2 * n_kv], refs[1 + 2 * n_kv]
    tq = q_ref.shape[0]
    hpp = hpg // n_part
    qs = [jnp.concatenate([q_ref[:, h * HD:(h + 1) * HD] for h in range(pi * hpp, (pi + 1) * hpp)], axis=0)
          for pi in range(n_part)]
    m_rows = hpp * tq

    segs = []
    for sidx in range(n_kv):
        k_ref, v_ref = kv[2 * sidx], kv[2 * sidx + 1]
        t = min(tk, k_ref.shape[0])
        segs.append((k_ref, v_ref, t, k_ref.shape[0] // t))

    def tiles(seg, it):
        k_ref, v_ref, t, _ = seg
        rs = pl.ds(pl.multiple_of(it * t, t), t)
        return k_ref[rs, :], jnp.concatenate([v_ref[rs, :], jnp.ones((t, HD), BF16)], axis=1)

    def row_max(s):
        return jnp.max(s, axis=-1, keepdims=True)

    def sweep(step, carry, first_done):
        for si, seg in enumerate(segs):
            lo = 1 if (si == 0 and first_done) else 0
            if seg[3] - lo == 1:
                carry = step(*tiles(seg, lo), carry)
            elif seg[3] - lo > 1:
                carry = lax.fori_loop(lo, seg[3], lambda it, c, seg=seg: step(*tiles(seg, it), c), carry)
        return carry

    def write(accs):
        for pi, acc in enumerate(accs):
            out = acc[:, :HD] / acc[:, HD:]
            for hh in range(hpp):
                h = pi * hpp + hh
                o_ref[:, h * HD:(h + 1) * HD] = out[hh * tq:(hh + 1) * tq].astype(o_ref.dtype)

    k0, v0 = tiles(segs[0], 0)
    ms, accs = [], []
    for q in qs:
        s = _dot_nt(q, k0)
        ms.append(row_max(s))
        accs.append(_dot(jnp.exp2(s - ms[-1]).astype(BF16), v0))

    def fast_step(k, v1, accs):
        ss = [_dot_nt(q, k) for q in qs]
        return [acc + _dot(jnp.exp2(s - m).astype(BF16), v1) for s, m, acc in zip(ss, ms, accs)]

    accs = sweep(fast_step, accs, True)
    write(accs)
    overflowed = jnp.zeros((), F32)
    for acc in accs:
        overflowed = jnp.maximum(overflowed, jnp.max(jnp.where(jnp.isfinite(acc), 0.0, 1.0)))

    @pl.when(overflowed > 0.0)
    def _():
        def exact_step(k, v1, carry):
            out = []
            for q, (m, acc) in zip(qs, carry):
                s = _dot_nt(q, k)
                m_new = jnp.maximum(m, row_max(s))
                out.append((m_new, jnp.exp2(m - m_new) * acc + _dot(jnp.exp2(s - m_new).astype(BF16), v1)))
            return out

        init = [(jnp.full((m_rows, 1), -jnp.inf, F32), jnp.zeros((m_rows, 2 * HD), F32)) for _ in qs]
        write([acc for _, acc in sweep(exact_step, init, False)])


def _attention(q, kvs, dims, tq_rows, tq=256, tk=2048, n_part=2):
    rows = q.shape[0]
    hpg = dims.q_heads // dims.kv_heads
    gw = hpg * HD
    tq = min(tq, tq_rows)
    nqb = tq_rows // tq
    in_specs = [pl.BlockSpec((tq, gw), lambda b, g, i: (b * nqb + i, g))]
    args = [q]
    for k, v_src, v_blk, s_len in kvs:
        in_specs.append(pl.BlockSpec((s_len, HD), lambda b, g, i: (b, g)))
        in_specs.append(pl.BlockSpec((s_len, HD), lambda b, g, i, v_blk=v_blk: (b, v_blk + g)))
        args += [k, v_src]
    return pl.pallas_call(
        functools.partial(_attn_kernel, n_kv=len(kvs), tk=tk, hpg=hpg, n_part=math.gcd(n_part, hpg)),
        grid=(rows // tq_rows, dims.kv_heads, nqb),
        in_specs=in_specs,
        out_specs=pl.BlockSpec((tq, gw), lambda b, g, i: (b * nqb + i, g)),
        out_shape=jax.ShapeDtypeStruct((rows, dims.w_gqa), BF16),
        compiler_params=_cparams(("parallel", "parallel", "arbitrary")),
        name="attention",
    )(*args)


N_SEG = 8


def _lru_kernel(x_ref, gb_ref, cw_ref, cb_ref, wg_ref, bg_ref, lam_ref, h0_ref,
                y_ref, hlast_ref, xnat, xi, a_f, u_f, a_b, u_b, *, rc):
    ts = x_ref.shape[0]
    sl = ts // N_SEG
    cw, cb = cw_ref[...], cb_ref[...]
    wg, bg = wg_ref[...], bg_ref[...]
    sp = jax.nn.softplus(-lam_ref[...])

    def copy_body(r, carry):
        rows = pl.ds(pl.multiple_of(r * rc, rc), rc)
        xnat[rows, :] = x_ref[rows, :].astype(F32)
        return carry

    lax.fori_loop(0, ts // rc, copy_body, 0)

    def interleave_body(t, carry):
        xi[pl.ds(pl.multiple_of((t + 2) * N_SEG, N_SEG), N_SEG), :] = xnat[pl.ds(t, N_SEG, stride=sl), :]
        return carry

    lax.fori_loop(0, sl, interleave_body, 0, unroll=8)
    sub = lax.broadcasted_iota(jnp.int32, (N_SEG, LANES), 0)
    for t_dst, t_src in ((-2, sl - 2), (-1, sl - 1)):
        v = xi[(t_src + 2) * N_SEG:(t_src + 3) * N_SEG, :]
        xi[(t_dst + 2) * N_SEG:(t_dst + 3) * N_SEG, :] = jnp.where(sub == 0, 0.0, pltpu.roll(v, 1, 0))
    v = xi[2 * N_SEG:3 * N_SEG, :]
    xi[(sl + 2) * N_SEG:(sl + 3) * N_SEG, :] = jnp.where(sub == N_SEG - 1, 0.0, pltpu.roll(v, N_SEG - 1, 0))

    def gates_body(r, carry):
        start = pl.multiple_of(r * rc, rc)
        rows = pl.ds(start, rc)
        win = xi[pl.ds(start, rc + (LRU_CONV - 1) * N_SEG), :]
        xc = cb + sum(win[j * N_SEG:j * N_SEG + rc] * cw[j:j + 1] for j in range(LRU_CONV))
        gts = _sigmoid(_dot(xc.astype(BF16), wg) + bg)
        for d, (a_ref, u_ref) in enumerate(((a_f, u_f), (a_b, u_b))):
            rg = gts[:, (2 * d) * LANES:(2 * d + 1) * LANES]
            ig = gts[:, (2 * d + 1) * LANES:(2 * d + 2) * LANES]
            log_a = -LRU_C * rg * sp[d:d + 1]
            a = jnp.exp(log_a)
            scale = (1.0 + a) * jnp.sqrt(-jnp.tanh(0.5 * log_a))
            a_ref[rows, :] = a
            u_ref[rows, :] = scale * (ig * xc)
        return carry

    lax.fori_loop(0, ts // rc, gates_body, 0)

    zeros = jnp.zeros((N_SEG, LANES), F32)
    ones = jnp.ones((N_SEG, LANES), F32)

    def scan_step(a_ref, u_ref, tt, h, acc):
        rows = pl.ds(pl.multiple_of(tt * N_SEG, N_SEG), N_SEG)
        a = a_ref[rows, :]
        h = a * h + u_ref[rows, :]
        acc = a * acc
        u_ref[rows, :] = h
        a_ref[rows, :] = acc
        return h, acc

    def scan_body(t, carry):
        hf, af, hb, ab = carry
        hf, af = scan_step(a_f, u_f, t, hf, af)
        hb, ab = scan_step(a_b, u_b, sl - 1 - t, hb, ab)
        return hf, af, hb, ab

    hf, af, hb, ab = lax.fori_loop(0, sl, scan_body, (zeros, ones, zeros, ones), unroll=8)

    h0 = h0_ref[...]
    carry_f, carry_b = [h0[0:1]], [h0[1:2]]
    for s in range(N_SEG):
        carry_f.append(hf[s:s + 1] + af[s:s + 1] * carry_f[-1])
        sb = N_SEG - 1 - s
        carry_b.append(hb[sb:sb + 1] + ab[sb:sb + 1] * carry_b[-1])
    hlast_ref[...] = jnp.concatenate([carry_f[-1], carry_b[-1], jnp.zeros((6, LANES), F32)], axis=0)
    enter_f = jnp.concatenate(carry_f[:N_SEG], axis=0)
    enter_b = jnp.concatenate(carry_b[:N_SEG][::-1], axis=0)

    def stitch_body(t, carry):
        rows = pl.ds(pl.multiple_of(t * N_SEG, N_SEG), N_SEG)
        h = (u_f[rows, :] + a_f[rows, :] * enter_f) + (u_b[rows, :] + a_b[rows, :] * enter_b)
        xnat[pl.ds(t, N_SEG, stride=sl), :] = h
        return carry

    lax.fori_loop(0, sl, stitch_body, 0, unroll=8)

    def out_body(r, carry):
        rows = pl.ds(pl.multiple_of(r * rc, rc), rc)
        y_ref[rows, :] = (jax.nn.gelu(gb_ref[rows, :].astype(F32)) * xnat[rows, :]).astype(y_ref.dtype)
        return carry

    lax.fori_loop(0, ts // rc, out_body, 0)


def _lru(p, conv_w, conv_b, w_gates, b_gates, lam, h0, dims, ts):
    rows = p.shape[0]
    w = dims.w_lru
    nct = w // LANES
    xb, gb = dims.off_lx // LANES, dims.off_lg // LANES
    nb = rows // ts
    return pl.pallas_call(
        functools.partial(_lru_kernel, rc=min(256, ts)),
        grid=(nb, nct),
        in_specs=[
            pl.BlockSpec((ts, LANES), lambda b, c: (b, xb + c)),
            pl.BlockSpec((ts, LANES), lambda b, c: (b, gb + c)),
            pl.BlockSpec((LRU_CONV, LANES), lambda b, c: (0, c)),
            pl.BlockSpec((1, LANES), lambda b, c: (0, c)),
            pl.BlockSpec((None, LANES, 4 * LANES), lambda b, c: (c, 0, 0)),
            pl.BlockSpec((None, 1, 4 * LANES), lambda b, c: (c, 0, 0)),
            pl.BlockSpec((2, LANES), lambda b, c: (0, c)),
            pl.BlockSpec((None, 8, LANES), lambda b, c: (b, 0, c)),
        ],
        out_specs=[
            pl.BlockSpec((ts, LANES), lambda b, c: (b, c)),
            pl.BlockSpec((None, 8, LANES), lambda b, c: (b, 0, c)),
        ],
        out_shape=[
            jax.ShapeDtypeStruct((rows, w), BF16),
            jax.ShapeDtypeStruct((nb, 8, w), F32),
        ],
        scratch_shapes=[pltpu.VMEM((ts, LANES), F32), pltpu.VMEM((ts + (LRU_CONV - 1) * N_SEG, LANES), F32)]
        + [pltpu.VMEM((ts, LANES), F32)] * 4,
        compiler_params=_cparams(("parallel", "parallel")),
        name="lru",
    )(p, p, conv_w, conv_b, w_gates, b_gates, lam, h0)


def _final_norm_kernel(x_ref, g_ref, o_ref):
    x = x_ref[...]
    o_ref[...] = x * lax.rsqrt(jnp.mean(x * x, axis=-1, keepdims=True) + EPS) * g_ref[...]


def _final_norm(x, g, tr=512):
    rows, d = x.shape
    tr = min(tr, rows)
    return pl.pallas_call(
        _final_norm_kernel,
        grid=(rows // tr,),
        in_specs=[pl.BlockSpec((tr, d), lambda i: (i, 0)), pl.BlockSpec((1, d), lambda i: (0, 0))],
        out_specs=pl.BlockSpec((tr, d), lambda i: (i, 0)),
        out_shape=jax.ShapeDtypeStruct((rows, d), F32),
        compiler_params=_cparams(("parallel",)),
        name="final_norm",
    )(x, g.reshape(1, d))


def _rope_tables(dims):
    t = dims.seq
    rows = t // dims.grid_w
    row = jnp.repeat(jnp.arange(rows, dtype=F32), dims.grid_w)
    col = jnp.tile(jnp.arange(dims.grid_w, dtype=F32), rows)
    axis_dim = HD // 2
    inv_freq = ROPE_THETA ** (-jnp.arange(0, axis_dim, 2, dtype=F32) / axis_dim)
    ar, ac = row[:, None] * inv_freq, col[:, None] * inv_freq
    cos = jnp.concatenate([jnp.cos(ar)] * 2 + [jnp.cos(ac)] * 2, axis=-1)
    sin = jnp.concatenate([-jnp.sin(ar), jnp.sin(ar), -jnp.sin(ac), jnp.sin(ac)], axis=-1)
    return cos, sin


def _split_w_in(w_in, dims):
    nh = dims.gdn_heads
    kv = dims.kv_heads * HD
    sizes = (3 * dims.w_gdn, dims.w_gdn, nh, nh, nh, nh, dims.w_gqa, kv, kv, dims.w_lru, dims.w_lru)
    bounds = np.cumsum(sizes)[:-1].tolist()
    qkv, z, b_f, b_b, a_f, a_b, gq, gk, gv, lx, lg = jnp.split(w_in, bounds, axis=-1)
    main = jnp.concatenate([qkv, z, gq, gk, gv, lx, lg], axis=-1).astype(BF16)
    pad = jnp.zeros(w_in.shape[:2] + (LANES - 4 * nh,), w_in.dtype)
    gate = jnp.concatenate([b_f, b_b, a_f, a_b, pad], axis=-1).astype(BF16)
    return main, gate


def _gate_rows(v, dims):
    nh = dims.gdn_heads
    flat = v.reshape(v.shape[0], 1, 2 * nh).astype(F32)
    return jnp.pad(flat, ((0, 0), (0, 0), (2 * nh, LANES - 4 * nh)))


def _lru_gate_weights(gate_w, gate_b, dims):
    depth = gate_w.shape[0]
    bs = dims.w_lru // LRU_BLOCKS
    per = LANES // bs
    nct = dims.w_lru // LANES
    w = gate_w.reshape(depth, 4, nct, per, bs, bs)
    eye = jnp.eye(per, dtype=gate_w.dtype)
    dense = jnp.einsum("lgcpde,pq->lcpdgqe", w, eye).reshape(depth, nct, LANES, 4 * LANES)
    bias = gate_b.reshape(depth, 4, nct, LANES).transpose(0, 2, 1, 3).reshape(depth, nct, 1, 4 * LANES)
    return dense.astype(BF16), bias.astype(F32)


def _forward(dims, x, c, ctx, c_ctx, ada_w, ada_b, norm1_g, norm2_g, w_in, gdn_conv_w, gdn_a_log, gdn_dt_bias,
             gdn_norm_g, q_norm_g, k_norm_g, lru_conv_w, lru_conv_b, lru_gate_w, lru_gate_b, lru_lambda,
             w_out, ffn_w_up, ffn_conv_w, ffn_conv_b, ffn_w_down, final_norm_g):
    d, b, t, ct, depth = dims.d, dims.batch, dims.seq, dims.ctx, dims.depth
    f, fp = dims.ffn, dims.ffn_pad
    nh = dims.gdn_heads
    tm = min(1024, t)

    xl = x.reshape(b * t, d)
    xc = ctx.reshape(b * ct, d)

    cond8 = jnp.concatenate([c, c_ctx[None, :], jnp.zeros((8 - b - 1, d), F32)], axis=0)
    mods = _adaln(cond8, ada_w, ada_b).reshape(depth, 8, N_MOD, 1, d)
    blocks_per_seq = t // tm
    mrow_l = lambda i: i // blocks_per_seq
    mrow_c = lambda i: b

    w_main, w_gate = _split_w_in(w_in, dims)
    alog_rows, dt_rows = _gate_rows(gdn_a_log, dims), _gate_rows(gdn_dt_bias, dims)
    lru_w, lru_b = _lru_gate_weights(lru_gate_w, lru_gate_b, dims)
    wo = w_out.astype(BF16)
    wo_parts = (wo[:, :dims.w_gdn], wo[:, dims.w_gdn:dims.w_gdn + dims.w_gqa], wo[:, dims.w_gdn + dims.w_gqa:])
    padc = ((0, 0), (0, 0), (0, fp - f))
    wu_g = jnp.pad(ffn_w_up[:, :, :f], padc).astype(BF16)
    wu_u = jnp.pad(ffn_w_up[:, :, f:], padc).astype(BF16)
    cw_g, cw_u = jnp.pad(ffn_conv_w[:, :, :f], padc), jnp.pad(ffn_conv_w[:, :, f:], padc)
    cb = ffn_conv_b[:, None, :]
    cb_g, cb_u = jnp.pad(cb[:, :, :f], padc), jnp.pad(cb[:, :, f:], padc)
    wd = jnp.pad(ffn_w_down, ((0, 0), (0, fp - f), (0, 0))).astype(BF16)
    cos, sin = _rope_tables(dims)
    v_blk = dims.off_v // HD

    s_zero = jnp.zeros((b, nh, HD, HD), F32)
    h_zero = jnp.zeros((b, 8, dims.w_lru), F32)

    for l in range(depth):
        ctx_out = l < depth - 1
        p_l, g_l = _in_proj(xl, mods, l, mrow_l, norm1_g[l], w_main, w_gate, tm)
        p_c, g_c = _in_proj(xc, mods, l, mrow_c, norm1_g[l], w_main, w_gate, tm)

        qkv_l = _gdn_prep(p_l, gdn_conv_w[l], dims, t)
        qkv_c = _gdn_prep(p_c, gdn_conv_w[l], dims, ct)
        o_l, o_c = [], []
        for di, rev in enumerate((False, True)):
            oc, sc = _gdn_scan(qkv_c, g_c, alog_rows[l], dt_rows[l], s_zero, dims, ct, rev, di)
            ol, _ = _gdn_scan(qkv_l, g_l, alog_rows[l], dt_rows[l], sc, dims, t, rev, di)
            o_l.append(ol)
            o_c.append(oc)
        ya_l = _gdn_out(o_l[0], o_l[1], p_l, gdn_norm_g[l], dims)

        q_l, k_l = _qk_prep(p_l, q_norm_g[l], k_norm_g[l], cos, sin, dims, True)
        q_c, k_c = _qk_prep(p_c, q_norm_g[l], k_norm_g[l], cos, sin, dims, False)
        yb_l = _attention(q_l, [(k_l, p_l, v_blk, t), (k_c, p_c, v_blk, ct)], dims, t)

        yc_c, h_c = _lru(p_c, lru_conv_w[l], lru_conv_b[l][None, :], lru_w[l], lru_b[l], lru_lambda[l], h_zero, dims, ct)
        yc_l, _ = _lru(p_l, lru_conv_w[l], lru_conv_b[l][None, :], lru_w[l], lru_b[l], lru_lambda[l], h_c, dims, t)

        ffn_args = (wu_g, wu_u, cw_g, cw_u, cb_g, cb_u)
        xl = _proj_res((ya_l, yb_l, yc_l), wo_parts, xl, mods, l, 2, mrow_l, tm, 1024)
        a_l = _ffn_up(xl, mods, l, mrow_l, norm2_g[l], *ffn_args, t, tm)
        xl = _proj_res((a_l,), (wd,), xl, mods, l, 5, mrow_l, tm, 512)
        if ctx_out:
            ya_c = _gdn_out(o_c[0], o_c[1], p_c, gdn_norm_g[l], dims)
            yb_c = _attention(q_c, [(k_c, p_c, v_blk, ct)], dims, ct)
            xc = _proj_res((ya_c, yb_c, yc_c), wo_parts, xc, mods, l, 2, mrow_c, tm, 1024)
            a_c = _ffn_up(xc, mods, l, mrow_c, norm2_g[l], *ffn_args, ct, tm)
            xc = _proj_res((a_c,), (wd,), xc, mods, l, 5, mrow_c, tm, 512)

    return _final_norm(xl, final_norm_g).reshape(b, t, d)


def kernel(x, c, ctx, c_ctx, ada_w, ada_b, norm1_g, norm2_g, w_in, gdn_conv_w, gdn_a_log, gdn_dt_bias, gdn_norm_g, q_norm_g, k_norm_g, lru_conv_w, lru_conv_b, lru_gate_w, lru_gate_b, lru_lambda, w_out, ffn_w_up, ffn_conv_w, ffn_conv_b, ffn_w_down, final_norm_g):
    b, t, d = x.shape
    dims = Dims(d=d, batch=b, seq=t, ctx=ctx.shape[1], depth=ada_w.shape[0], grid_w=64)
    return _forward(dims, x, c, ctx, c_ctx, ada_w, ada_b, norm1_g, norm2_g, w_in, gdn_conv_w, gdn_a_log,
                    gdn_dt_bias, gdn_norm_g, q_norm_g, k_norm_g, lru_conv_w, lru_conv_b, lru_gate_w,
                    lru_gate_b, lru_lambda, w_out, ffn_w_up, ffn_conv_w, ffn_conv_b, ffn_w_down, final_norm_g)
```

```python
import functools
import math
from typing import NamedTuple

import jax
import jax.numpy as jnp
import numpy as np
from jax import lax
from jax.experimental import pallas as pl
from jax.experimental.pallas import tpu as pltpu

F32 = jnp.float32
BF16 = jnp.bfloat16
EPS = 1e-6
HD = 128
LANES = 128
N_MOD = 6
GDN_CONV = 4
LRU_CONV = 4
LRU_C = 8.0
LRU_BLOCKS = 8
ROPE_THETA = 10000.0
Q_SCALE = HD ** -0.5 * math.log2(math.e)
CH = 64
CONV_PAD = 8
VMEM_LIMIT = 56 * 1024 * 1024


class Dims(NamedTuple):
    d: int
    batch: int
    seq: int
    ctx: int
    depth: int
    grid_w: int

    @property
    def w_gdn(self):
        return self.d // 4

    @property
    def w_gqa(self):
        return self.d // 2

    @property
    def w_lru(self):
        return self.d - self.w_gdn - self.w_gqa

    @property
    def gdn_heads(self):
        return self.w_gdn // HD

    @property
    def q_heads(self):
        return self.w_gqa // HD

    @property
    def kv_heads(self):
        return self.q_heads // 4

    @property
    def ffn(self):
        return ((8 * self.d // 3 + 127) // 128) * 128

    @property
    def ffn_pad(self):
        return ((self.ffn + 511) // 512) * 512

    @property
    def p_cols(self):
        return 4 * self.w_gdn + self.w_gqa + 2 * self.kv_heads * HD + 2 * self.w_lru

    @property
    def off_z(self):
        return 3 * self.w_gdn

    @property
    def off_q(self):
        return 4 * self.w_gdn

    @property
    def off_k(self):
        return self.off_q + self.w_gqa

    @property
    def off_v(self):
        return self.off_k + self.kv_heads * HD

    @property
    def off_lx(self):
        return self.off_v + self.kv_heads * HD

    @property
    def off_lg(self):
        return self.off_lx + self.w_lru


def _cparams(sem, vmem=VMEM_LIMIT):
    return pltpu.CompilerParams(dimension_semantics=sem, vmem_limit_bytes=vmem)


def _dot(a, b):
    return jnp.dot(a, b, preferred_element_type=F32)


def _dot_exact(a, b):
    return jnp.dot(a, b, preferred_element_type=F32, precision=lax.Precision.HIGHEST)


def _dot_nt(a, b):
    return lax.dot_general(a, b, (((1,), (1,)), ((), ())), preferred_element_type=F32)


def _dot_tn(a, b):
    return lax.dot_general(a, b, (((0,), (0,)), ((), ())), preferred_element_type=F32)


def _sigmoid(x):
    return 0.5 * jnp.tanh(0.5 * x) + 0.5


def _silu(x):
    return x * _sigmoid(x)


def _rms_mod(x, g, sh, sc):
    ms = jnp.mean(x * x, axis=-1, keepdims=True)
    return (x * lax.rsqrt(ms + EPS)) * g * (1.0 + sc) + sh


def _adaln_kernel(c_ref, w_ref, b_ref, o_ref):
    s = _silu(c_ref[...]).astype(BF16)
    o_ref[...] = _dot(s, w_ref[...].astype(BF16)) + b_ref[...]


def _adaln(cond8, ada_w, ada_b, tn=1024):
    depth, d, n = ada_w.shape
    tn = min(tn, n)
    return pl.pallas_call(
        _adaln_kernel,
        grid=(depth, n // tn),
        in_specs=[
            pl.BlockSpec((8, d), lambda l, j: (0, 0)),
            pl.BlockSpec((None, d, tn), lambda l, j: (l, 0, j)),
            pl.BlockSpec((None, 1, tn), lambda l, j: (l, 0, j)),
        ],
        out_specs=pl.BlockSpec((None, 8, tn), lambda l, j: (l, 0, j)),
        out_shape=jax.ShapeDtypeStruct((depth, 8, n), F32),
        compiler_params=_cparams(("parallel", "parallel")),
        name="adaln",
    )(cond8, ada_w, ada_b.reshape(depth, 1, n))


def _mod_spec(layer, k, mrow, d, tn=None):
    if tn is None:
        return pl.BlockSpec((None, None, None, 1, d), lambda i, j: (layer, mrow(i), k, 0, 0))
    return pl.BlockSpec((None, None, None, 1, tn), lambda i, j: (layer, mrow(i), k, 0, j))


def _in_proj_kernel(x_ref, g_ref, sh_ref, sc_ref, w_ref, wg_ref, p_ref, gate_ref, h_scr, *, rc):
    tm = x_ref.shape[0]

    @pl.when(pl.program_id(1) == 0)
    def _():
        def body(r, carry):
            rows = pl.ds(pl.multiple_of(r * rc, rc), rc)
            h = _rms_mod(x_ref[rows, :], g_ref[...], sh_ref[...], sc_ref[...]).astype(BF16)
            h_scr[rows, :] = h
            gate_ref[rows, :] = _dot(h, wg_ref[...])
            return carry

        lax.fori_loop(0, tm // rc, body, 0)

    p_ref[...] = _dot(h_scr[...], w_ref[...]).astype(p_ref.dtype)


def _in_proj(x, mods, layer, mrow, norm_g, w_main, w_gate, tm):
    rows, d = x.shape
    n = w_main.shape[-1]
    tm = min(tm, rows)
    tn = next(t for t in (1536, 1024, 512, 256, LANES) if n % t == 0)
    assert rows % tm == 0
    return pl.pallas_call(
        functools.partial(_in_proj_kernel, rc=min(256, tm)),
        grid=(rows // tm, n // tn),
        in_specs=[
            pl.BlockSpec((tm, d), lambda i, j: (i, 0)),
            pl.BlockSpec((1, d), lambda i, j: (0, 0)),
            _mod_spec(layer, 0, mrow, d),
            _mod_spec(layer, 1, mrow, d),
            pl.BlockSpec((None, d, tn), lambda i, j: (layer, 0, j)),
            pl.BlockSpec((None, d, LANES), lambda i, j: (layer, 0, 0)),
        ],
        out_specs=[
            pl.BlockSpec((tm, tn), lambda i, j: (i, j)),
            pl.BlockSpec((tm, LANES), lambda i, j: (i, 0)),
        ],
        out_shape=[
            jax.ShapeDtypeStruct((rows, n), BF16),
            jax.ShapeDtypeStruct((rows, LANES), F32),
        ],
        scratch_shapes=[pltpu.VMEM((tm, d), BF16)],
        compiler_params=_cparams(("parallel", "arbitrary")),
        name="in_proj",
    )(x, norm_g.reshape(1, d), mods, mods, w_main, w_gate)


def _proj_res_kernel(*refs, n_in):
    ys, ws = refs[:n_in], refs[n_in:2 * n_in]
    x_ref, g_ref, o_ref = refs[2 * n_in:]
    acc = _dot(ys[0][...], ws[0][...])
    for y, w in zip(ys[1:], ws[1:]):
        acc = acc + _dot(y[...], w[...])
    o_ref[...] = x_ref[...] + g_ref[...] * acc


def _proj_res(ys, ws, x, mods, layer, k_gate, mrow, tm, tn):
    rows, d = x.shape
    tm = min(tm, rows)
    n_in = len(ys)
    in_specs = [pl.BlockSpec((tm, y.shape[1]), lambda i, j: (i, 0)) for y in ys]
    in_specs += [pl.BlockSpec((None, w.shape[1], tn), lambda i, j: (layer, 0, j)) for w in ws]
    in_specs += [pl.BlockSpec((tm, tn), lambda i, j: (i, j)), _mod_spec(layer, k_gate, mrow, d, tn)]
    return pl.pallas_call(
        functools.partial(_proj_res_kernel, n_in=n_in),
        grid=(rows // tm, d // tn),
        in_specs=in_specs,
        out_specs=pl.BlockSpec((tm, tn), lambda i, j: (i, j)),
        out_shape=jax.ShapeDtypeStruct((rows, d), F32),
        compiler_params=_cparams(("parallel", "parallel")),
        name="proj_res",
    )(*ys, *ws, x, mods)


FFN_HALO = 16


def _ffn_up_kernel(xp_ref, x_ref, xn_ref, g_ref, sh_ref, sc_ref, wg_ref, wu_ref,
                   cwg_ref, cwu_ref, cbg_ref, cbu_ref, o_ref, h_scr, *, seq_len, rc):
    tm = x_ref.shape[0]
    i = pl.program_id(0)

    @pl.when(pl.program_id(1) == 0)
    def _():
        def norm(x):
            return _rms_mod(x, g_ref[...], sh_ref[...], sc_ref[...]).astype(BF16)

        bps = max(seq_len // tm, 1)
        h_scr[0:FFN_HALO, :] = jnp.where(i % bps == 0, 0.0, norm(xp_ref[...])).astype(BF16)
        h_scr[FFN_HALO + tm:2 * FFN_HALO + tm, :] = jnp.where(i % bps == bps - 1, 0.0, norm(xn_ref[...])).astype(BF16)

        def body(r, carry):
            src = pl.ds(pl.multiple_of(r * rc, rc), rc)
            dst = pl.ds(pl.multiple_of(FFN_HALO + r * rc, FFN_HALO), rc)
            h_scr[dst, :] = norm(x_ref[src, :])
            return carry

        lax.fori_loop(0, tm // rc, body, 0)

    h = h_scr[...]
    several_seqs = seq_len < tm
    if several_seqs:
        pos = lax.rem(lax.broadcasted_iota(jnp.int32, (tm, 1), 0), seq_len)
        has_prev = pos != 0
        has_next = pos != seq_len - 1

    def conv_half(w_ref, cw_ref, cb_ref):
        u = _dot(h, w_ref[...])
        up = pltpu.roll(u, 1, 0)[FFN_HALO:FFN_HALO + tm]
        uc = u[FFN_HALO:FFN_HALO + tm]
        un = pltpu.roll(u, tm + 2 * FFN_HALO - 1, 0)[FFN_HALO:FFN_HALO + tm]
        if several_seqs:
            up = jnp.where(has_prev, up, 0.0)
            un = jnp.where(has_next, un, 0.0)
        cw = cw_ref[...]
        return up * cw[0:1] + uc * cw[1:2] + un * cw[2:3] + cb_ref[...]

    yg = conv_half(wg_ref, cwg_ref, cbg_ref)
    yu = conv_half(wu_ref, cwu_ref, cbu_ref)
    o_ref[...] = (_silu(yg) * yu).astype(o_ref.dtype)


def _ffn_up(x, mods, layer, mrow, norm_g, w_gate, w_up, cw_g, cw_u, cb_g, cb_u, seq_len, tm, tn=512):
    rows, d = x.shape
    fp = w_gate.shape[-1]
    tm = min(tm, rows)
    hb = tm // FFN_HALO
    last = rows // FFN_HALO - 1
    return pl.pallas_call(
        functools.partial(_ffn_up_kernel, seq_len=seq_len, rc=min(256, tm)),
        grid=(rows // tm, fp // tn),
        in_specs=[
            pl.BlockSpec((FFN_HALO, d), lambda i, j: (jnp.maximum(i * hb - 1, 0), 0)),
            pl.BlockSpec((tm, d), lambda i, j: (i, 0)),
            pl.BlockSpec((FFN_HALO, d), lambda i, j: (jnp.minimum((i + 1) * hb, last), 0)),
            pl.BlockSpec((1, d), lambda i, j: (0, 0)),
            _mod_spec(layer, 3, mrow, d),
            _mod_spec(layer, 4, mrow, d),
            pl.BlockSpec((None, d, tn), lambda i, j: (layer, 0, j)),
            pl.BlockSpec((None, d, tn), lambda i, j: (layer, 0, j)),
            pl.BlockSpec((None, 3, tn), lambda i, j: (layer, 0, j)),
            pl.BlockSpec((None, 3, tn), lambda i, j: (layer, 0, j)),
            pl.BlockSpec((None, 1, tn), lambda i, j: (layer, 0, j)),
            pl.BlockSpec((None, 1, tn), lambda i, j: (layer, 0, j)),
        ],
        out_specs=pl.BlockSpec((tm, tn), lambda i, j: (i, j)),
        out_shape=jax.ShapeDtypeStruct((rows, fp), BF16),
        scratch_shapes=[pltpu.VMEM((tm + 2 * FFN_HALO, d), BF16)],
        compiler_params=_cparams(("parallel", "arbitrary")),
        name="ffn_up",
    )(x, x, x, norm_g.reshape(1, d), mods, mods, w_gate, w_up, cw_g, cw_u, cb_g, cb_u)


def _fill_padded(xpad_ref, x_ref):
    ts = x_ref.shape[0]
    zeros = jnp.zeros((CONV_PAD, xpad_ref.shape[1]), F32)
    xpad_ref[0:CONV_PAD, :] = zeros
    xpad_ref[CONV_PAD + ts:2 * CONV_PAD + ts, :] = zeros
    rc = min(512, ts)

    def body(r, carry):
        src = pl.ds(pl.multiple_of(r * rc, rc), rc)
        dst = pl.ds(pl.multiple_of(CONV_PAD + r * rc, CONV_PAD), rc)
        xpad_ref[dst, :] = x_ref[src, :].astype(F32)
        return carry

    lax.fori_loop(0, ts // rc, body, 0)


def _conv4_rows(xpad_ref, start, rc, cw):
    n = rc + 2 * CONV_PAD
    win = xpad_ref[pl.ds(pl.multiple_of(start, CONV_PAD), n), :]
    taps = [pltpu.roll(win, shift % n, 0) if shift else win for shift in (2, 1, 0, -1)]
    return sum(tap[CONV_PAD:CONV_PAD + rc] * cw[j:j + 1] for j, tap in enumerate(taps))


def _gdn_prep_kernel(x_ref, cw_ref, o_ref, xpad, *, n_heads, rc):
    ts = x_ref.shape[0]
    j = pl.program_id(1)
    _fill_padded(xpad, x_ref)
    cw = cw_ref[...]
    is_qk = j < 2 * n_heads
    fac = jnp.where(j < n_heads, HD ** -0.5, 1.0).astype(F32)

    def body(r, carry):
        start = r * rc
        y = _silu(_conv4_rows(xpad, start, rc, cw))
        yn = y * lax.rsqrt(jnp.sum(y * y, axis=-1, keepdims=True) + EPS) * fac
        o_ref[pl.ds(pl.multiple_of(start, rc), rc), :] = jnp.where(is_qk, yn, y).astype(o_ref.dtype)
        return carry

    lax.fori_loop(0, ts // rc, body, 0)


def _gdn_prep(p, conv_w, dims, ts):
    rows = p.shape[0]
    nb = 3 * dims.gdn_heads
    return pl.pallas_call(
        functools.partial(_gdn_prep_kernel, n_heads=dims.gdn_heads, rc=min(512, ts)),
        grid=(rows // ts, nb),
        in_specs=[
            pl.BlockSpec((ts, HD), lambda b, j: (b, j)),
            pl.BlockSpec((GDN_CONV, HD), lambda b, j: (0, j)),
        ],
        out_specs=pl.BlockSpec((ts, HD), lambda b, j: (b, j)),
        out_shape=jax.ShapeDtypeStruct((rows, nb * HD), BF16),
        scratch_shapes=[pltpu.VMEM((ts + 2 * CONV_PAD, HD), F32)],
        compiler_params=_cparams(("parallel", "parallel")),
        name="gdn_prep",
    )(p, conv_w)


def _gdn_masks(n_heads, reverse):
    n = n_heads * CH
    r = np.arange(n)[:, None]
    c = np.arange(n)[None, :]
    same = (r // CH) == (c // CH)
    after = (r <= c) if reverse else (r >= c)
    incl = same & after
    strict = incl & (r != c)
    masks = [incl, strict, same, r == c]
    s = 1
    while s < CH:
        pair = (r // (2 * s)) == (c // (2 * s))
        hi, lo = (r // s) % 2 == 1, (c // s) % 2 == 0
        if reverse:
            hi, lo = (r // s) % 2 == 0, (c // s) % 2 == 1
        masks.append(pair & hi & lo & same)
        s *= 2
    return np.stack(masks).astype(np.float32)


def _chunk_cumsum(x, reverse):
    seg = x.shape[0]
    pos = lax.broadcasted_iota(jnp.int32, x.shape, 0) % CH
    k = 1
    while k < CH:
        if reverse:
            x = x + jnp.where(pos < CH - k, pltpu.roll(x, seg - k, 0), 0.0)
        else:
            x = x + jnp.where(pos >= k, pltpu.roll(x, k, 0), 0.0)
        k *= 2
    return x


def _gdn_scan_kernel(q_ref, k_ref, v_ref, g_ref, alog_ref, dt_ref, m_ref, s0_ref,
                     o_ref, sfin_ref, s_scr, g_scr, be_scr, *, n_heads, reverse, d_idx, group):
    seg = q_ref.shape[0]
    nch = seg // CH
    n = n_heads * CH
    sidx = pl.program_id(1)

    @pl.when(sidx == 0)
    def _():
        s_scr[...] = s0_ref[...]

    g_all = g_ref[...]
    log_a = -jnp.exp(alog_ref[...]) * jax.nn.softplus(g_all + dt_ref[...])
    g_cum = _chunk_cumsum(log_a, reverse)
    beta = _sigmoid(g_all)
    for h in range(n_heads):
        cb, cg = d_idx * n_heads + h, (2 + d_idx) * n_heads + h
        g_scr[:, h * LANES:(h + 1) * LANES] = jnp.broadcast_to(g_cum[:, cg:cg + 1], (seg, LANES))
        be_scr[:, h * LANES:(h + 1) * LANES] = jnp.broadcast_to(beta[:, cb:cb + 1], (seg, LANES))

    incl, strict, eye = m_ref[0], m_ref[1], m_ref[3]
    last = 0 if reverse else CH - 1

    def stack(x):
        return jnp.concatenate([x[:, h * LANES:(h + 1) * LANES] for h in range(n_heads)], axis=0)

    def body(t, carry):
        order = [t * group + j for j in range(group)]
        rss = [pl.ds(pl.multiple_of(((nch - 1 - c) if reverse else c) * CH, CH), CH) for c in order]
        gs = [stack(g_scr[rs, :]) for rs in rss]
        bes = [stack(be_scr[rs, :]) for rs in rss]
        gls = [jnp.concatenate([jnp.broadcast_to(g[h * CH + last:h * CH + last + 1], (CH, LANES))
                                for h in range(n_heads)], axis=0) for g in gs]
        egs = [jnp.exp(g) for g in gs]
        g2s = [jnp.concatenate([g] * (n // LANES), axis=1) if n >= LANES else g[:, :n] for g in gs]
        grows = [jnp.sum(g2 * eye, axis=0, keepdims=True) for g2 in g2s]
        dms = [jnp.exp(jnp.minimum(g2 - grow, 0.0)) * incl for g2, grow in zip(g2s, grows)]
        qss = [stack(q_ref[rs, :]) for rs in rss]
        kss = [stack(k_ref[rs, :]) for rs in rss]
        ksfs = [ks.astype(F32) for ks in kss]
        kbs = [ksf * be for ksf, be in zip(ksfs, bes)]
        grams = [_dot_nt(jnp.concatenate([kb.astype(BF16), qs], axis=0), ks) for kb, qs, ks in zip(kbs, qss, kss)]
        lms = [gram[:n] * dm * strict for gram, dm in zip(grams, dms)]
        attns = [(gram[n:] * dm).astype(BF16) for gram, dm in zip(grams, dms)]
        xs = [eye - lm * m_ref[4] for lm in lms]
        lvl = 5
        s = 2
        while s < CH:
            mask = m_ref[lvl]
            xbs = [x.astype(BF16) for x in xs]
            offs = [(lm * mask).astype(BF16) for lm in lms]
            tmp = [_dot(xb, off).astype(BF16) for xb, off in zip(xbs, offs)]
            xs = [x - _dot(t1, xb) for x, t1, xb in zip(xs, tmp, xbs)]
            lvl += 1
            s *= 2
        rhss = [jnp.concatenate([stack(v_ref[rs, :]).astype(F32) * be, kb * eg], axis=1)
                for rs, be, kb, eg in zip(rss, bes, kbs, egs)]
        sols = [rhs + _dot((x - eye).astype(BF16), rhs.astype(BF16)) for x, rhs in zip(xs, rhss)]
        qds = [(qs.astype(F32) * eg).astype(BF16) for qs, eg in zip(qss, egs)]
        kds = [(ksf * jnp.exp(gl - g)).astype(BF16) for ksf, gl, g in zip(ksfs, gls, gs)]
        for j in range(group):
            rs, gl, sol, qd, kd, attn = rss[j], gls[j], sols[j], qds[j], kds[j], attns[j]
            u, wb = sol[:, :HD], sol[:, HD:].astype(BF16)
            ws_parts, qs_parts = [], []
            for h in range(n_heads):
                sh = s_scr[h].astype(BF16)
                hs = slice(h * CH, (h + 1) * CH)
                both = _dot(jnp.concatenate([wb[hs], qd[hs]], axis=0), sh)
                ws_parts.append(both[:CH])
                qs_parts.append(both[CH:])
            vnew = u - jnp.concatenate(ws_parts, axis=0)
            vnb = vnew.astype(BF16)
            o = jnp.concatenate(qs_parts, axis=0) + _dot(attn, vnb)
            for h in range(n_heads):
                hs = slice(h * CH, (h + 1) * CH)
                decay = jnp.exp(gl[h * CH:h * CH + 1, :])
                s_scr[h] = s_scr[h] * decay + _dot_tn(kd[hs], vnb[hs])
                o_ref[rs, h * HD:(h + 1) * HD] = o[hs]
        return carry

    lax.fori_loop(0, nch // group, body, 0)

    @pl.when(sidx == pl.num_programs(1) - 1)
    def _():
        sfin_ref[...] = s_scr[...]


def _gdn_scan(qkv, gates, alog_row, dt_row, s0, dims, ts, reverse, d_idx):
    rows = qkv.shape[0]
    nh = dims.gdn_heads
    w = nh * HD
    seg = min(1024, ts)
    nseg = ts // seg
    masks = jnp.asarray(_gdn_masks(nh, reverse))

    def rb(b, s):
        return b * nseg + ((nseg - 1 - s) if reverse else s)

    return pl.pallas_call(
        functools.partial(_gdn_scan_kernel, n_heads=nh, reverse=reverse, d_idx=d_idx, group=math.gcd(4, seg // CH)),
        grid=(rows // ts, nseg),
        in_specs=[
            pl.BlockSpec((seg, w), lambda b, s: (rb(b, s), 0)),
            pl.BlockSpec((seg, w), lambda b, s: (rb(b, s), 1)),
            pl.BlockSpec((seg, w), lambda b, s: (rb(b, s), 2)),
            pl.BlockSpec((seg, LANES), lambda b, s: (rb(b, s), 0)),
            pl.BlockSpec((1, LANES), lambda b, s: (0, 0)),
            pl.BlockSpec((1, LANES), lambda b, s: (0, 0)),
            pl.BlockSpec(masks.shape, lambda b, s: (0, 0, 0)),
            pl.BlockSpec((None, nh, HD, HD), lambda b, s: (b, 0, 0, 0)),
        ],
        out_specs=[
            pl.BlockSpec((seg, w), lambda b, s: (rb(b, s), 0)),
            pl.BlockSpec((None, nh, HD, HD), lambda b, s: (b, 0, 0, 0)),
        ],
        out_shape=[
            jax.ShapeDtypeStruct((rows, w), F32),
            jax.ShapeDtypeStruct(s0.shape, F32),
        ],
        scratch_shapes=[
            pltpu.VMEM((nh, HD, HD), F32),
            pltpu.VMEM((seg, w), F32),
            pltpu.VMEM((seg, w), F32),
        ],
        compiler_params=_cparams(("parallel", "arbitrary")),
        name="gdn_scan_rev" if reverse else "gdn_scan_fwd",
    )(qkv, qkv, qkv, gates, alog_row, dt_row, masks, s0)


def _gdn_out_kernel(of_ref, ob_ref, z_ref, g_ref, y_ref, *, n_heads):
    g = g_ref[...]
    for h in range(n_heads):
        cs = slice(h * HD, (h + 1) * HD)
        o = of_ref[:, cs] + ob_ref[:, cs]
        y = o * lax.rsqrt(jnp.mean(o * o, axis=-1, keepdims=True) + EPS) * g
        y_ref[:, cs] = (y * _silu(z_ref[:, cs].astype(F32))).astype(y_ref.dtype)


def _gdn_out(o_f, o_b, p, norm_g, dims, tr=512):
    rows, w = o_f.shape
    tr = min(tr, rows)
    zb = dims.off_z // w
    return pl.pallas_call(
        functools.partial(_gdn_out_kernel, n_heads=dims.gdn_heads),
        grid=(rows // tr,),
        in_specs=[
            pl.BlockSpec((tr, w), lambda i: (i, 0)),
            pl.BlockSpec((tr, w), lambda i: (i, 0)),
            pl.BlockSpec((tr, w), lambda i: (i, zb)),
            pl.BlockSpec((1, HD), lambda i: (0, 0)),
        ],
        out_specs=pl.BlockSpec((tr, w), lambda i: (i, 0)),
        out_shape=jax.ShapeDtypeStruct((rows, w), BF16),
        compiler_params=_cparams(("parallel",)),
        name="gdn_out",
    )(o_f, o_b, p, norm_g.reshape(1, HD))


def _qk_prep_kernel(q_ref, k_ref, qg_ref, kg_ref, cos_ref, sin_ref, qo_ref, ko_ref, *, rope):
    if rope:
        cos, sin = cos_ref[...], sin_ref[...]

    def one(x, g, scale):
        x = x.astype(F32)
        y = x * lax.rsqrt(jnp.mean(x * x, axis=-1, keepdims=True) + EPS) * g
        if rope:
            y = y * cos + pltpu.roll(y, HD // 2, 1) * sin
        return y * scale

    for h in range(q_ref.shape[1] // HD):
        cs = slice(h * HD, (h + 1) * HD)
        qo_ref[:, cs] = one(q_ref[:, cs], qg_ref[...], Q_SCALE).astype(qo_ref.dtype)
    for h in range(k_ref.shape[1] // HD):
        cs = slice(h * HD, (h + 1) * HD)
        ko_ref[:, cs] = one(k_ref[:, cs], kg_ref[...], 1.0).astype(ko_ref.dtype)


def _qk_prep(p, q_g, k_g, cos, sin, dims, rope, tr=512):
    rows = p.shape[0]
    wq, wk = dims.w_gqa, dims.kv_heads * HD
    tr = min(tr, rows)
    nt = cos.shape[0] // tr if rope else 1
    tab = pl.BlockSpec((tr, HD), (lambda i: (i % nt, 0)) if rope else (lambda i: (0, 0)))
    return pl.pallas_call(
        functools.partial(_qk_prep_kernel, rope=rope),
        grid=(rows // tr,),
        in_specs=[
            pl.BlockSpec((tr, wq), lambda i: (i, dims.off_q // wq)),
            pl.BlockSpec((tr, wk), lambda i: (i, dims.off_k // wk)),
            pl.BlockSpec((1, HD), lambda i: (0, 0)),
            pl.BlockSpec((1, HD), lambda i: (0, 0)),
            tab, tab,
        ],
        out_specs=[
            pl.BlockSpec((tr, wq), lambda i: (i, 0)),
            pl.BlockSpec((tr, wk), lambda i: (i, 0)),
        ],
        out_shape=[
            jax.ShapeDtypeStruct((rows, wq), BF16),
            jax.ShapeDtypeStruct((rows, wk), BF16),
        ],
        compiler_params=_cparams(("parallel",)),
        name="qk_prep",
    )(p, p, q_g.reshape(1, HD), k_g.reshape(1, HD), cos, sin)


def _attn_kernel(*refs, n_kv, tk, hpg, n_part):
    q_ref, kv, o_ref = refs[0], refs[1:1 + 2 * n_kv], refs[1 + 2 * n_kv]
    tq = q_ref.shape[0]
    hpp = hpg // n_part
    qs = [jnp.concatenate([q_ref[:, h * HD:(h + 1) * HD] for h in range(pi * hpp, (pi + 1) * hpp)], axis=0)
          for pi in range(n_part)]
    m_rows = hpp * tq

    segs = []
    for sidx in range(n_kv):
        k_ref, v_ref = kv[2 * sidx], kv[2 * sidx + 1]
        t = min(tk, k_ref.shape[0])
        segs.append((k_ref, v_ref, t, k_ref.shape[0] // t))

    def tiles(seg, it):
        k_ref, v_ref, t, _ = seg
        rs = pl.ds(pl.multiple_of(it * t, t), t)
        return k_ref[rs, :], jnp.concatenate([v_ref[rs, :], jnp.ones((t, HD), BF16)], axis=1)

    def row_max(s):
        return jnp.max(s, axis=-1, keepdims=True)

    def sweep(step, carry, first_done):
        for si, seg in enumerate(segs):
            lo = 1 if (si == 0 and first_done) else 0
            if seg[3] - lo == 1:
                carry = step(*tiles(seg, lo), carry)
            elif seg[3] - lo > 1:
                carry = lax.fori_loop(lo, seg[3], lambda it, c, seg=seg: step(*tiles(seg, it), c), carry)
        return carry

    def write(accs):
        for pi, acc in enumerate(accs):
            out = acc[:, :HD] / acc[:, HD:]
            for hh in range(hpp):
                h = pi * hpp + hh
                o_ref[:, h * HD:(h + 1) * HD] = out[hh * tq:(hh + 1) * tq].astype(o_ref.dtype)

    k0, v0 = tiles(segs[0], 0)
    ms, accs = [], []
    for q in qs:
        s = _dot_nt(q, k0)
        ms.append(row_max(s))
        accs.append(_dot(jnp.exp2(s - ms[-1]).astype(BF16), v0))

    def fast_step(k, v1, accs):
        ss = [_dot_nt(q, k) for q in qs]
        return [acc + _dot(jnp.exp2(s - m).astype(BF16), v1) for s, m, acc in zip(ss, ms, accs)]

    accs = sweep(fast_step, accs, True)
    write(accs)
    overflowed = jnp.zeros((), F32)
    for acc in accs:
        overflowed = jnp.maximum(overflowed, jnp.max(jnp.where(jnp.isfinite(acc), 0.0, 1.0)))

    @pl.when(overflowed > 0.0)
    def _():
        def exact_step(k, v1, carry):
            out = []
            for q, (m, acc) in zip(qs, carry):
                s = _dot_nt(q, k)
                m_new = jnp.maximum(m, row_max(s))
                out.append((m_new, jnp.exp2(m - m_new) * acc + _dot(jnp.exp2(s - m_new).astype(BF16), v1)))
            return out

        init = [(jnp.full((m_rows, 1), -jnp.inf, F32), jnp.zeros((m_rows, 2 * HD), F32)) for _ in qs]
        write([acc for _, acc in sweep(exact_step, init, False)])


def _attention(q, kvs, dims, tq_rows, tq=512, tk=2048, n_part=2):
    rows = q.shape[0]
    hpg = dims.q_heads // dims.kv_heads
    gw = hpg * HD
    tq = min(tq, tq_rows)
    nqb = tq_rows // tq
    in_specs = [pl.BlockSpec((tq, gw), lambda b, g, i: (b * nqb + i, g))]
    args = [q]
    for k, v_src, v_blk, s_len in kvs:
        in_specs.append(pl.BlockSpec((s_len, HD), lambda b, g, i: (b, g)))
        in_specs.append(pl.BlockSpec((s_len, HD), lambda b, g, i, v_blk=v_blk: (b, v_blk + g)))
        args += [k, v_src]
    return pl.pallas_call(
        functools.partial(_attn_kernel, n_kv=len(kvs), tk=tk, hpg=hpg, n_part=math.gcd(n_part, hpg)),
        grid=(rows // tq_rows, dims.kv_heads, nqb),
        in_specs=in_specs,
        out_specs=pl.BlockSpec((tq, gw), lambda b, g, i: (b * nqb + i, g)),
        out_shape=jax.ShapeDtypeStruct((rows, dims.w_gqa), BF16),
        compiler_params=_cparams(("parallel", "parallel", "arbitrary")),
        name="attention",
    )(*args)


N_SEG = 8


def _lru_kernel(x_ref, gb_ref, cw_ref, cb_ref, wg_ref, bg_ref, lam_ref, h0_ref,
                y_ref, hlast_ref, xnat, xi, a_f, u_f, a_b, u_b, *, rc):
    ts = x_ref.shape[0]
    sl = ts // N_SEG
    cw, cb = cw_ref[...], cb_ref[...]
    wg, bg = wg_ref[...], bg_ref[...]
    sp = jax.nn.softplus(-lam_ref[...])

    def copy_body(r, carry):
        rows = pl.ds(pl.multiple_of(r * rc, rc), rc)
        xnat[rows, :] = x_ref[rows, :].astype(F32)
        return carry

    lax.fori_loop(0, ts // rc, copy_body, 0)

    def interleave_body(t, carry):
        xi[pl.ds(pl.multiple_of((t + 2) * N_SEG, N_SEG), N_SEG), :] = xnat[pl.ds(t, N_SEG, stride=sl), :]
        return carry

    lax.fori_loop(0, sl, interleave_body, 0, unroll=8)
    sub = lax.broadcasted_iota(jnp.int32, (N_SEG, LANES), 0)
    for t_dst, t_src in ((-2, sl - 2), (-1, sl - 1)):
        v = xi[(t_src + 2) * N_SEG:(t_src + 3) * N_SEG, :]
        xi[(t_dst + 2) * N_SEG:(t_dst + 3) * N_SEG, :] = jnp.where(sub == 0, 0.0, pltpu.roll(v, 1, 0))
    v = xi[2 * N_SEG:3 * N_SEG, :]
    xi[(sl + 2) * N_SEG:(sl + 3) * N_SEG, :] = jnp.where(sub == N_SEG - 1, 0.0, pltpu.roll(v, N_SEG - 1, 0))

    def gates_body(r, carry):
        start = pl.multiple_of(r * rc, rc)
        rows = pl.ds(start, rc)
        win = xi[pl.ds(start, rc + (LRU_CONV - 1) * N_SEG), :]
        xc = cb + sum(win[j * N_SEG:j * N_SEG + rc] * cw[j:j + 1] for j in range(LRU_CONV))
        gts = _sigmoid(_dot(xc.astype(BF16), wg) + bg)
        for d, (a_ref, u_ref) in enumerate(((a_f, u_f), (a_b, u_b))):
            rg = gts[:, (2 * d) * LANES:(2 * d + 1) * LANES]
            ig = gts[:, (2 * d + 1) * LANES:(2 * d + 2) * LANES]
            log_a = -LRU_C * rg * sp[d:d + 1]
            a = jnp.exp(log_a)
            scale = (1.0 + a) * jnp.sqrt(-jnp.tanh(0.5 * log_a))
            a_ref[rows, :] = a
            u_ref[rows, :] = scale * (ig * xc)
        return carry

    lax.fori_loop(0, ts // rc, gates_body, 0)

    zeros = jnp.zeros((N_SEG, LANES), F32)
    ones = jnp.ones((N_SEG, LANES), F32)

    def scan_step(a_ref, u_ref, tt, h, acc):
        rows = pl.ds(pl.multiple_of(tt * N_SEG, N_SEG), N_SEG)
        a = a_ref[rows, :]
        h = a * h + u_ref[rows, :]
        acc = a * acc
        u_ref[rows, :] = h
        a_ref[rows, :] = acc
        return h, acc

    def scan_body(t, carry):
        hf, af, hb, ab = carry
        hf, af = scan_step(a_f, u_f, t, hf, af)
        hb, ab = scan_step(a_b, u_b, sl - 1 - t, hb, ab)
        return hf, af, hb, ab

    hf, af, hb, ab = lax.fori_loop(0, sl, scan_body, (zeros, ones, zeros, ones), unroll=8)

    h0 = h0_ref[...]
    carry_f, carry_b = [h0[0:1]], [h0[1:2]]
    for s in range(N_SEG):
        carry_f.append(hf[s:s + 1] + af[s:s + 1] * carry_f[-1])
        sb = N_SEG - 1 - s
        carry_b.append(hb[sb:sb + 1] + ab[sb:sb + 1] * carry_b[-1])
    hlast_ref[...] = jnp.concatenate([carry_f[-1], carry_b[-1], jnp.zeros((6, LANES), F32)], axis=0)
    enter_f = jnp.concatenate(carry_f[:N_SEG], axis=0)
    enter_b = jnp.concatenate(carry_b[:N_SEG][::-1], axis=0)

    def stitch_body(t, carry):
        rows = pl.ds(pl.multiple_of(t * N_SEG, N_SEG), N_SEG)
        h = (u_f[rows, :] + a_f[rows, :] * enter_f) + (u_b[rows, :] + a_b[rows, :] * enter_b)
        xnat[pl.ds(t, N_SEG, stride=sl), :] = h
        return carry

    lax.fori_loop(0, sl, stitch_body, 0, unroll=8)

    def out_body(r, carry):
        rows = pl.ds(pl.multiple_of(r * rc, rc), rc)
        y_ref[rows, :] = (jax.nn.gelu(gb_ref[rows, :].astype(F32)) * xnat[rows, :]).astype(y_ref.dtype)
        return carry

    lax.fori_loop(0, ts // rc, out_body, 0)


def _lru(p, conv_w, conv_b, w_gates, b_gates, lam, h0, dims, ts):
    rows = p.shape[0]
    w = dims.w_lru
    nct = w // LANES
    xb, gb = dims.off_lx // LANES, dims.off_lg // LANES
    nb = rows // ts
    return pl.pallas_call(
        functools.partial(_lru_kernel, rc=min(512, ts)),
        grid=(nb, nct),
        in_specs=[
            pl.BlockSpec((ts, LANES), lambda b, c: (b, xb + c)),
            pl.BlockSpec((ts, LANES), lambda b, c: (b, gb + c)),
            pl.BlockSpec((LRU_CONV, LANES), lambda b, c: (0, c)),
            pl.BlockSpec((1, LANES), lambda b, c: (0, c)),
            pl.BlockSpec((None, LANES, 4 * LANES), lambda b, c: (c, 0, 0)),
            pl.BlockSpec((None, 1, 4 * LANES), lambda b, c: (c, 0, 0)),
            pl.BlockSpec((2, LANES), lambda b, c: (0, c)),
            pl.BlockSpec((None, 8, LANES), lambda b, c: (b, 0, c)),
        ],
        out_specs=[
            pl.BlockSpec((ts, LANES), lambda b, c: (b, c)),
            pl.BlockSpec((None, 8, LANES), lambda b, c: (b, 0, c)),
        ],
        out_shape=[
            jax.ShapeDtypeStruct((rows, w), BF16),
            jax.ShapeDtypeStruct((nb, 8, w), F32),
        ],
        scratch_shapes=[pltpu.VMEM((ts, LANES), F32), pltpu.VMEM((ts + (LRU_CONV - 1) * N_SEG, LANES), F32)]
        + [pltpu.VMEM((ts, LANES), F32)] * 4,
        compiler_params=_cparams(("parallel", "parallel")),
        name="lru",
    )(p, p, conv_w, conv_b, w_gates, b_gates, lam, h0)


def _final_norm_kernel(x_ref, g_ref, o_ref):
    x = x_ref[...]
    o_ref[...] = x * lax.rsqrt(jnp.mean(x * x, axis=-1, keepdims=True) + EPS) * g_ref[...]


def _final_norm(x, g, tr=512):
    rows, d = x.shape
    tr = min(tr, rows)
    return pl.pallas_call(
        _final_norm_kernel,
        grid=(rows // tr,),
        in_specs=[pl.BlockSpec((tr, d), lambda i: (i, 0)), pl.BlockSpec((1, d), lambda i: (0, 0))],
        out_specs=pl.BlockSpec((tr, d), lambda i: (i, 0)),
        out_shape=jax.ShapeDtypeStruct((rows, d), F32),
        compiler_params=_cparams(("parallel",)),
        name="final_norm",
    )(x, g.reshape(1, d))


def _rope_tables(dims):
    t = dims.seq
    rows = t // dims.grid_w
    row = jnp.repeat(jnp.arange(rows, dtype=F32), dims.grid_w)
    col = jnp.tile(jnp.arange(dims.grid_w, dtype=F32), rows)
    axis_dim = HD // 2
    inv_freq = ROPE_THETA ** (-jnp.arange(0, axis_dim, 2, dtype=F32) / axis_dim)
    ar, ac = row[:, None] * inv_freq, col[:, None] * inv_freq
    cos = jnp.concatenate([jnp.cos(ar), jnp.cos(ac)] * 2, axis=-1)
    sin = jnp.concatenate([-jnp.sin(ar), -jnp.sin(ac), jnp.sin(ar), jnp.sin(ac)], axis=-1)
    return cos, sin


def _pair_major(v):
    shape = v.shape
    v = v.reshape(shape[:-1] + (shape[-1] // HD, 2, 2, HD // 4))
    return jnp.swapaxes(v, -3, -2).reshape(shape)


def _split_w_in(w_in, dims):
    nh = dims.gdn_heads
    kv = dims.kv_heads * HD
    sizes = (3 * dims.w_gdn, dims.w_gdn, nh, nh, nh, nh, dims.w_gqa, kv, kv, dims.w_lru, dims.w_lru)
    bounds = np.cumsum(sizes)[:-1].tolist()
    qkv, z, b_f, b_b, a_f, a_b, gq, gk, gv, lx, lg = jnp.split(w_in, bounds, axis=-1)
    main = jnp.concatenate([qkv, z, _pair_major(gq), _pair_major(gk), gv, lx, lg], axis=-1).astype(BF16)
    pad = jnp.zeros(w_in.shape[:2] + (LANES - 4 * nh,), w_in.dtype)
    gate = jnp.concatenate([b_f, b_b, a_f, a_b, pad], axis=-1).astype(BF16)
    return main, gate


def _gate_rows(v, dims):
    nh = dims.gdn_heads
    flat = v.reshape(v.shape[0], 1, 2 * nh).astype(F32)
    return jnp.pad(flat, ((0, 0), (0, 0), (2 * nh, LANES - 4 * nh)))


def _lru_gate_weights(gate_w, gate_b, dims):
    depth = gate_w.shape[0]
    bs = dims.w_lru // LRU_BLOCKS
    per = LANES // bs
    nct = dims.w_lru // LANES
    w = gate_w.reshape(depth, 4, nct, per, bs, bs)
    eye = jnp.eye(per, dtype=gate_w.dtype)
    dense = jnp.einsum("lgcpde,pq->lcpdgqe", w, eye).reshape(depth, nct, LANES, 4 * LANES)
    bias = gate_b.reshape(depth, 4, nct, LANES).transpose(0, 2, 1, 3).reshape(depth, nct, 1, 4 * LANES)
    return dense.astype(BF16), bias.astype(F32)


def _forward(dims, x, c, ctx, c_ctx, ada_w, ada_b, norm1_g, norm2_g, w_in, gdn_conv_w, gdn_a_log, gdn_dt_bias,
             gdn_norm_g, q_norm_g, k_norm_g, lru_conv_w, lru_conv_b, lru_gate_w, lru_gate_b, lru_lambda,
             w_out, ffn_w_up, ffn_conv_w, ffn_conv_b, ffn_w_down, final_norm_g):
    d, b, t, ct, depth = dims.d, dims.batch, dims.seq, dims.ctx, dims.depth
    f, fp = dims.ffn, dims.ffn_pad
    nh = dims.gdn_heads
    tm = min(1024, t)

    xl = x.reshape(b * t, d)
    xc = ctx.reshape(b * ct, d)

    cond8 = jnp.concatenate([c, c_ctx[None, :], jnp.zeros((8 - b - 1, d), F32)], axis=0)
    mods = _adaln(cond8, ada_w, ada_b).reshape(depth, 8, N_MOD, 1, d)
    blocks_per_seq = t // tm
    mrow_l = lambda i: i // blocks_per_seq
    mrow_c = lambda i: b

    w_main, w_gate = _split_w_in(w_in, dims)
    alog_rows, dt_rows = _gate_rows(gdn_a_log, dims), _gate_rows(gdn_dt_bias, dims)
    lru_w, lru_b = _lru_gate_weights(lru_gate_w, lru_gate_b, dims)
    wo = w_out.astype(BF16)
    wo_parts = (wo[:, :dims.w_gdn], wo[:, dims.w_gdn:dims.w_gdn + dims.w_gqa], wo[:, dims.w_gdn + dims.w_gqa:])
    padc = ((0, 0), (0, 0), (0, fp - f))
    wu_g = jnp.pad(ffn_w_up[:, :, :f], padc).astype(BF16)
    wu_u = jnp.pad(ffn_w_up[:, :, f:], padc).astype(BF16)
    cw_g, cw_u = jnp.pad(ffn_conv_w[:, :, :f], padc), jnp.pad(ffn_conv_w[:, :, f:], padc)
    cb = ffn_conv_b[:, None, :]
    cb_g, cb_u = jnp.pad(cb[:, :, :f], padc), jnp.pad(cb[:, :, f:], padc)
    wd = jnp.pad(ffn_w_down, ((0, 0), (0, fp - f), (0, 0))).astype(BF16)
    cos, sin = _rope_tables(dims)
    v_blk = dims.off_v // HD

    s_zero = jnp.zeros((b, nh, HD, HD), F32)
    h_zero = jnp.zeros((b, 8, dims.w_lru), F32)

    for l in range(depth):
        ctx_out = l < depth - 1
        p_l, g_l = _in_proj(xl, mods, l, mrow_l, norm1_g[l], w_main, w_gate, tm)
        p_c, g_c = _in_proj(xc, mods, l, mrow_c, norm1_g[l], w_main, w_gate, tm)

        qkv_l = _gdn_prep(p_l, gdn_conv_w[l], dims, t)
        qkv_c = _gdn_prep(p_c, gdn_conv_w[l], dims, ct)
        o_l, o_c = [], []
        for di, rev in enumerate((False, True)):
            oc, sc = _gdn_scan(qkv_c, g_c, alog_rows[l], dt_rows[l], s_zero, dims, ct, rev, di)
            ol, _ = _gdn_scan(qkv_l, g_l, alog_rows[l], dt_rows[l], sc, dims, t, rev, di)
            o_l.append(ol)
            o_c.append(oc)
        ya_l = _gdn_out(o_l[0], o_l[1], p_l, gdn_norm_g[l], dims)

        qg, kg = _pair_major(q_norm_g[l]), _pair_major(k_norm_g[l])
        q_l, k_l = _qk_prep(p_l, qg, kg, cos, sin, dims, True)
        q_c, k_c = _qk_prep(p_c, qg, kg, cos, sin, dims, False)
        yb_l = _attention(q_l, [(k_l, p_l, v_blk, t), (k_c, p_c, v_blk, ct)], dims, t)

        yc_c, h_c = _lru(p_c, lru_conv_w[l], lru_conv_b[l][None, :], lru_w[l], lru_b[l], lru_lambda[l], h_zero, dims, ct)
        yc_l, _ = _lru(p_l, lru_conv_w[l], lru_conv_b[l][None, :], lru_w[l], lru_b[l], lru_lambda[l], h_c, dims, t)

        ffn_args = (wu_g, wu_u, cw_g, cw_u, cb_g, cb_u)
        xl = _proj_res((ya_l, yb_l, yc_l), wo_parts, xl, mods, l, 2, mrow_l, tm, 1024)
        a_l = _ffn_up(xl, mods, l, mrow_l, norm2_g[l], *ffn_args, t, tm)
        xl = _proj_res((a_l,), (wd,), xl, mods, l, 5, mrow_l, tm, 512)
        if ctx_out:
            ya_c = _gdn_out(o_c[0], o_c[1], p_c, gdn_norm_g[l], dims)
            yb_c = _attention(q_c, [(k_c, p_c, v_blk, ct)], dims, ct)
            xc = _proj_res((ya_c, yb_c, yc_c), wo_parts, xc, mods, l, 2, mrow_c, tm, 1024)
            a_c = _ffn_up(xc, mods, l, mrow_c, norm2_g[l], *ffn_args, ct, tm)
            xc = _proj_res((a_c,), (wd,), xc, mods, l, 5, mrow_c, tm, 512)

    return _final_norm(xl, final_norm_g).reshape(b, t, d)


def kernel(x, c, ctx, c_ctx, ada_w, ada_b, norm1_g, norm2_g, w_in, gdn_conv_w, gdn_a_log, gdn_dt_bias, gdn_norm_g, q_norm_g, k_norm_g, lru_conv_w, lru_conv_b, lru_gate_w, lru_gate_b, lru_lambda, w_out, ffn_w_up, ffn_conv_w, ffn_conv_b, ffn_w_down, final_norm_g):
    b, t, d = x.shape
    dims = Dims(d=d, batch=b, seq=t, ctx=ctx.shape[1], depth=ada_w.shape[0], grid_w=64)
    return _forward(dims, x, c, ctx, c_ctx, ada_w, ada_b, norm1_g, norm2_g, w_in, gdn_conv_w, gdn_a_log,
                    gdn_dt_bias, gdn_norm_g, q_norm_g, k_norm_g, lru_conv_w, lru_conv_b, lru_gate_w,
                    lru_gate_b, lru_lambda, w_out, ffn_w_up, ffn_conv_w, ffn_conv_b, ffn_w_down, final_norm_g)
```

```python
import functools
import math
from typing import NamedTuple

import jax
import jax.numpy as jnp
import numpy as np
from jax import lax
from jax.experimental import pallas as pl
from jax.experimental.pallas import tpu as pltpu

F32 = jnp.float32
BF16 = jnp.bfloat16
EPS = 1e-6
HD = 128
LANES = 128
N_MOD = 6
GDN_CONV = 4
LRU_CONV = 4
LRU_C = 8.0
LRU_BLOCKS = 8
ROPE_THETA = 10000.0
Q_SCALE = HD ** -0.5 * math.log2(math.e)
CH = 64
CONV_PAD = 8
VMEM_LIMIT = 56 * 1024 * 1024


class Dims(NamedTuple):
    d: int
    batch: int
    seq: int
    ctx: int
    depth: int
    grid_w: int

    @property
    def w_gdn(self):
        return self.d // 4

    @property
    def w_gqa(self):
        return self.d // 2

    @property
    def w_lru(self):
        return self.d - self.w_gdn - self.w_gqa

    @property
    def gdn_heads(self):
        return self.w_gdn // HD

    @property
    def q_heads(self):
        return self.w_gqa // HD

    @property
    def kv_heads(self):
        return self.q_heads // 4

    @property
    def ffn(self):
        return ((8 * self.d // 3 + 127) // 128) * 128

    @property
    def ffn_pad(self):
        return ((self.ffn + 511) // 512) * 512

    @property
    def p_cols(self):
        return 4 * self.w_gdn + self.w_gqa + 2 * self.kv_heads * HD + 2 * self.w_lru

    @property
    def off_z(self):
        return 3 * self.w_gdn

    @property
    def off_q(self):
        return 4 * self.w_gdn

    @property
    def off_k(self):
        return self.off_q + self.w_gqa

    @property
    def off_v(self):
        return self.off_k + self.kv_heads * HD

    @property
    def off_lx(self):
        return self.off_v + self.kv_heads * HD

    @property
    def off_lg(self):
        return self.off_lx + self.w_lru


def _cparams(sem, vmem=VMEM_LIMIT):
    return pltpu.CompilerParams(dimension_semantics=sem, vmem_limit_bytes=vmem)


def _dot(a, b):
    return jnp.dot(a, b, preferred_element_type=F32)


def _dot_exact(a, b):
    return jnp.dot(a, b, preferred_element_type=F32, precision=lax.Precision.HIGHEST)


def _dot_nt(a, b):
    return lax.dot_general(a, b, (((1,), (1,)), ((), ())), preferred_element_type=F32)


def _dot_tn(a, b):
    return lax.dot_general(a, b, (((0,), (0,)), ((), ())), preferred_element_type=F32)


def _sigmoid(x):
    return 0.5 * jnp.tanh(0.5 * x) + 0.5


def _silu(x):
    return x * _sigmoid(x)


def _rms_mod(x, g, sh, sc):
    ms = jnp.mean(x * x, axis=-1, keepdims=True)
    return (x * lax.rsqrt(ms + EPS)) * g * (1.0 + sc) + sh


def _adaln_kernel(c_ref, w_ref, b_ref, o_ref):
    s = _silu(c_ref[...]).astype(BF16)
    o_ref[...] = _dot(s, w_ref[...].astype(BF16)) + b_ref[...]


def _adaln(cond8, ada_w, ada_b, tn=1024):
    depth, d, n = ada_w.shape
    tn = min(tn, n)
    return pl.pallas_call(
        _adaln_kernel,
        grid=(depth, n // tn),
        in_specs=[
            pl.BlockSpec((8, d), lambda l, j: (0, 0)),
            pl.BlockSpec((None, d, tn), lambda l, j: (l, 0, j)),
            pl.BlockSpec((None, 1, tn), lambda l, j: (l, 0, j)),
        ],
        out_specs=pl.BlockSpec((None, 8, tn), lambda l, j: (l, 0, j)),
        out_shape=jax.ShapeDtypeStruct((depth, 8, n), F32),
        compiler_params=_cparams(("parallel", "parallel")),
        name="adaln",
    )(cond8, ada_w, ada_b.reshape(depth, 1, n))


def _mod_spec(layer, k, mrow, d, tn=None):
    if tn is None:
        return pl.BlockSpec((None, None, None, 1, d), lambda i, j: (layer, mrow(i), k, 0, 0))
    return pl.BlockSpec((None, None, None, 1, tn), lambda i, j: (layer, mrow(i), k, 0, j))


def _in_proj_kernel(x_ref, g_ref, sh_ref, sc_ref, w_ref, wg_ref, p_ref, gate_ref, h_scr, *, rc):
    tm = x_ref.shape[0]

    @pl.when(pl.program_id(1) == 0)
    def _():
        def body(r, carry):
            rows = pl.ds(pl.multiple_of(r * rc, rc), rc)
            h = _rms_mod(x_ref[rows, :], g_ref[...], sh_ref[...], sc_ref[...]).astype(BF16)
            h_scr[rows, :] = h
            gate_ref[rows, :] = _dot(h, wg_ref[...])
            return carry

        lax.fori_loop(0, tm // rc, body, 0)

    p_ref[...] = _dot(h_scr[...], w_ref[...]).astype(p_ref.dtype)


def _in_proj(x, mods, layer, mrow, norm_g, w_main, w_gate, tm):
    rows, d = x.shape
    n = w_main.shape[-1]
    tm = min(tm, rows)
    tn = next(t for t in (1536, 1024, 512, 256, LANES) if n % t == 0)
    assert rows % tm == 0
    return pl.pallas_call(
        functools.partial(_in_proj_kernel, rc=min(256, tm)),
        grid=(rows // tm, n // tn),
        in_specs=[
            pl.BlockSpec((tm, d), lambda i, j: (i, 0)),
            pl.BlockSpec((1, d), lambda i, j: (0, 0)),
            _mod_spec(layer, 0, mrow, d),
            _mod_spec(layer, 1, mrow, d),
            pl.BlockSpec((None, d, tn), lambda i, j: (layer, 0, j)),
            pl.BlockSpec((None, d, LANES), lambda i, j: (layer, 0, 0)),
        ],
        out_specs=[
            pl.BlockSpec((tm, tn), lambda i, j: (i, j)),
            pl.BlockSpec((tm, LANES), lambda i, j: (i, 0)),
        ],
        out_shape=[
            jax.ShapeDtypeStruct((rows, n), BF16),
            jax.ShapeDtypeStruct((rows, LANES), F32),
        ],
        scratch_shapes=[pltpu.VMEM((tm, d), BF16)],
        compiler_params=_cparams(("parallel", "arbitrary")),
        name="in_proj",
    )(x, norm_g.reshape(1, d), mods, mods, w_main, w_gate)


def _proj_res_kernel(*refs, n_in):
    ys, ws = refs[:n_in], refs[n_in:2 * n_in]
    x_ref, g_ref, o_ref = refs[2 * n_in:]
    acc = _dot(ys[0][...], ws[0][...])
    for y, w in zip(ys[1:], ws[1:]):
        acc = acc + _dot(y[...], w[...])
    o_ref[...] = x_ref[...] + g_ref[...] * acc


def _proj_res(ys, ws, x, mods, layer, k_gate, mrow, tm, tn):
    rows, d = x.shape
    tm = min(tm, rows)
    n_in = len(ys)
    in_specs = [pl.BlockSpec((tm, y.shape[1]), lambda i, j: (i, 0)) for y in ys]
    offset = 0
    for y, w in zip(ys, ws):
        k = y.shape[1]
        rb = offset // k if w.shape[1] != k else 0
        assert w.shape[1] == k or offset % k == 0
        in_specs.append(pl.BlockSpec((None, k, tn), lambda i, j, rb=rb: (layer, rb, j)))
        offset += k
    in_specs += [pl.BlockSpec((tm, tn), lambda i, j: (i, j)), _mod_spec(layer, k_gate, mrow, d, tn)]
    return pl.pallas_call(
        functools.partial(_proj_res_kernel, n_in=n_in),
        grid=(rows // tm, d // tn),
        in_specs=in_specs,
        out_specs=pl.BlockSpec((tm, tn), lambda i, j: (i, j)),
        out_shape=jax.ShapeDtypeStruct((rows, d), F32),
        compiler_params=_cparams(("parallel", "parallel")),
        name="proj_res",
    )(*ys, *ws, x, mods)


FFN_HALO = 16


def _ffn_up_kernel(*refs, seq_len, rc, n_sub, with_norm):
    if with_norm:
        (xp_ref, x_ref, xn_ref, g_ref, sh_ref, sc_ref, wg_ref, wu_ref,
         cwg_ref, cwu_ref, cbg_ref, cbu_ref, o_ref, h_ref) = refs
    else:
        h_ref, wg_ref, wu_ref, cwg_ref, cwu_ref, cbg_ref, cbu_ref, o_ref = refs
    tm = o_ref.shape[0]
    tn = o_ref.shape[1] // n_sub

    if with_norm:
        i = pl.program_id(0)

        @pl.when(pl.program_id(1) == 0)
        def _():
            def norm(x):
                return _rms_mod(x, g_ref[...], sh_ref[...], sc_ref[...]).astype(BF16)

            bps = max(seq_len // tm, 1)
            h_ref[0:FFN_HALO, :] = jnp.where(i % bps == 0, 0.0, norm(xp_ref[...])).astype(BF16)
            h_ref[FFN_HALO + tm:2 * FFN_HALO + tm, :] = jnp.where(i % bps == bps - 1, 0.0, norm(xn_ref[...])).astype(BF16)

            def body(r, carry):
                src = pl.ds(pl.multiple_of(r * rc, rc), rc)
                dst = pl.ds(pl.multiple_of(FFN_HALO + r * rc, FFN_HALO), rc)
                h_ref[dst, :] = norm(x_ref[src, :])
                return carry

            lax.fori_loop(0, tm // rc, body, 0)

    h = h_ref[...]
    several_seqs = seq_len < tm
    if several_seqs:
        pos = lax.rem(lax.broadcasted_iota(jnp.int32, (tm, 1), 0), seq_len)
        has_prev = pos != 0
        has_next = pos != seq_len - 1

    def conv_half(w_ref, cw_ref, cb_ref, cs):
        u = _dot(h, w_ref[:, cs])
        up = pltpu.roll(u, 1, 0)[FFN_HALO:FFN_HALO + tm]
        uc = u[FFN_HALO:FFN_HALO + tm]
        un = pltpu.roll(u, tm + 2 * FFN_HALO - 1, 0)[FFN_HALO:FFN_HALO + tm]
        if several_seqs:
            up = jnp.where(has_prev, up, 0.0)
            un = jnp.where(has_next, un, 0.0)
        cw = cw_ref[:, cs]
        return up * cw[0:1] + uc * cw[1:2] + un * cw[2:3] + cb_ref[:, cs]

    for c in range(n_sub):
        cs = slice(c * tn, (c + 1) * tn)
        yg = conv_half(wg_ref, cwg_ref, cbg_ref, cs)
        yu = conv_half(wu_ref, cwu_ref, cbu_ref, cs)
        o_ref[:, cs] = (_silu(yg) * yu).astype(o_ref.dtype)


def _ffn_up(x, mods, layer, mrow, norm_g, w_gate, w_up, cw_g, cw_u, cb_g, cb_u, seq_len, tm, tn=512, n_sub=2):
    rows, d = x.shape
    fp = w_gate.shape[-1]
    tm = min(tm, rows)
    nb = rows // tm
    hb = tm // FFN_HALO
    last = rows // FFN_HALO - 1
    wide = tn * n_sub
    n_wide = fp // wide
    rest = fp - n_wide * wide
    hrows = tm + 2 * FFN_HALO

    def w_specs(width, col_block):
        cmap = lambda i, j: (layer, 0, col_block(j))
        return ([pl.BlockSpec((None, d, width), cmap)] * 2 + [pl.BlockSpec((None, 3, width), cmap)] * 2
                + [pl.BlockSpec((None, 1, width), cmap)] * 2)

    weights = (w_gate, w_up, cw_g, cw_u, cb_g, cb_u)
    a_wide, h_ext = pl.pallas_call(
        functools.partial(_ffn_up_kernel, seq_len=seq_len, rc=min(256, tm), n_sub=n_sub, with_norm=True),
        grid=(nb, n_wide),
        in_specs=[
            pl.BlockSpec((FFN_HALO, d), lambda i, j: (jnp.maximum(i * hb - 1, 0), 0)),
            pl.BlockSpec((tm, d), lambda i, j: (i, 0)),
            pl.BlockSpec((FFN_HALO, d), lambda i, j: (jnp.minimum((i + 1) * hb, last), 0)),
            pl.BlockSpec((1, d), lambda i, j: (0, 0)),
            _mod_spec(layer, 3, mrow, d),
            _mod_spec(layer, 4, mrow, d),
        ] + w_specs(wide, lambda j: j),
        out_specs=[
            pl.BlockSpec((tm, wide), lambda i, j: (i, j)),
            pl.BlockSpec((None, hrows, d), lambda i, j: (i, 0, 0)),
        ],
        out_shape=[
            jax.ShapeDtypeStruct((rows, n_wide * wide), BF16),
            jax.ShapeDtypeStruct((nb, hrows, d), BF16),
        ],
        compiler_params=_cparams(("parallel", "arbitrary")),
        name="ffn_up",
    )(x, x, x, norm_g.reshape(1, d), mods, mods, *weights)
    if not rest:
        return (a_wide,)
    assert (n_wide * wide) % rest == 0
    a_rest = pl.pallas_call(
        functools.partial(_ffn_up_kernel, seq_len=seq_len, rc=min(256, tm), n_sub=1, with_norm=False),
        grid=(nb, 1),
        in_specs=[pl.BlockSpec((None, hrows, d), lambda i, j: (i, 0, 0))] + w_specs(rest, lambda j: n_wide * wide // rest),
        out_specs=pl.BlockSpec((tm, rest), lambda i, j: (i, 0)),
        out_shape=jax.ShapeDtypeStruct((rows, rest), BF16),
        compiler_params=_cparams(("parallel", "arbitrary")),
        name="ffn_up_rest",
    )(h_ext, *weights)
    return a_wide, a_rest


def _fill_padded(xpad_ref, x_ref):
    ts = x_ref.shape[0]
    zeros = jnp.zeros((CONV_PAD, xpad_ref.shape[1]), F32)
    xpad_ref[0:CONV_PAD, :] = zeros
    xpad_ref[CONV_PAD + ts:2 * CONV_PAD + ts, :] = zeros
    rc = min(512, ts)

    def body(r, carry):
        src = pl.ds(pl.multiple_of(r * rc, rc), rc)
        dst = pl.ds(pl.multiple_of(CONV_PAD + r * rc, CONV_PAD), rc)
        xpad_ref[dst, :] = x_ref[src, :].astype(F32)
        return carry

    lax.fori_loop(0, ts // rc, body, 0)


def _conv4_rows(xpad_ref, start, rc, cw):
    n = rc + 2 * CONV_PAD
    win = xpad_ref[pl.ds(pl.multiple_of(start, CONV_PAD), n), :]
    taps = [pltpu.roll(win, shift % n, 0) if shift else win for shift in (2, 1, 0, -1)]
    return sum(tap[CONV_PAD:CONV_PAD + rc] * cw[j:j + 1] for j, tap in enumerate(taps))


def _gdn_prep_kernel(x_ref, cw_ref, o_ref, xpad, *, n_heads, rc):
    ts = x_ref.shape[0]
    j = pl.program_id(1)
    _fill_padded(xpad, x_ref)
    cw = cw_ref[...]
    is_qk = j < 2 * n_heads
    fac = jnp.where(j < n_heads, HD ** -0.5, 1.0).astype(F32)

    def body(r, carry):
        start = r * rc
        y = _silu(_conv4_rows(xpad, start, rc, cw))
        yn = y * lax.rsqrt(jnp.sum(y * y, axis=-1, keepdims=True) + EPS) * fac
        o_ref[pl.ds(pl.multiple_of(start, rc), rc), :] = jnp.where(is_qk, yn, y).astype(o_ref.dtype)
        return carry

    lax.fori_loop(0, ts // rc, body, 0)


def _gdn_prep(p, conv_w, dims, ts):
    rows = p.shape[0]
    nb = 3 * dims.gdn_heads
    return pl.pallas_call(
        functools.partial(_gdn_prep_kernel, n_heads=dims.gdn_heads, rc=min(512, ts)),
        grid=(rows // ts, nb),
        in_specs=[
            pl.BlockSpec((ts, HD), lambda b, j: (b, j)),
            pl.BlockSpec((GDN_CONV, HD), lambda b, j: (0, j)),
        ],
        out_specs=pl.BlockSpec((ts, HD), lambda b, j: (b, j)),
        out_shape=jax.ShapeDtypeStruct((rows, nb * HD), BF16),
        scratch_shapes=[pltpu.VMEM((ts + 2 * CONV_PAD, HD), F32)],
        compiler_params=_cparams(("parallel", "parallel")),
        name="gdn_prep",
    )(p, conv_w)


def _gdn_masks(n_heads, reverse):
    n = n_heads * CH
    r = np.arange(n)[:, None]
    c = np.arange(n)[None, :]
    same = (r // CH) == (c // CH)
    after = (r <= c) if reverse else (r >= c)
    incl = same & after
    strict = incl & (r != c)
    masks = [incl, strict, same, r == c]
    s = 1
    while s < CH:
        pair = (r // (2 * s)) == (c // (2 * s))
        hi, lo = (r // s) % 2 == 1, (c // s) % 2 == 0
        if reverse:
            hi, lo = (r // s) % 2 == 0, (c // s) % 2 == 1
        masks.append(pair & hi & lo & same)
        s *= 2
    return np.stack(masks).astype(np.float32)


def _chunk_cumsum(x, reverse):
    seg = x.shape[0]
    pos = lax.broadcasted_iota(jnp.int32, x.shape, 0) % CH
    k = 1
    while k < CH:
        if reverse:
            x = x + jnp.where(pos < CH - k, pltpu.roll(x, seg - k, 0), 0.0)
        else:
            x = x + jnp.where(pos >= k, pltpu.roll(x, k, 0), 0.0)
        k *= 2
    return x


def _gdn_scan_kernel(q_ref, k_ref, v_ref, g_ref, alog_ref, dt_ref, m_ref, s0_ref,
                     o_ref, sfin_ref, s_scr, g_scr, be_scr, *, n_heads, reverse, d_idx, group):
    seg = q_ref.shape[0]
    nch = seg // CH
    n = n_heads * CH
    sidx = pl.program_id(1)

    @pl.when(sidx == 0)
    def _():
        s_scr[...] = s0_ref[...]

    g_all = g_ref[...]
    log_a = -jnp.exp(alog_ref[...]) * jax.nn.softplus(g_all + dt_ref[...])
    g_cum = _chunk_cumsum(log_a, reverse)
    beta = _sigmoid(g_all)
    for h in range(n_heads):
        cb, cg = d_idx * n_heads + h, (2 + d_idx) * n_heads + h
        g_scr[:, h * LANES:(h + 1) * LANES] = jnp.broadcast_to(g_cum[:, cg:cg + 1], (seg, LANES))
        be_scr[:, h * LANES:(h + 1) * LANES] = jnp.broadcast_to(beta[:, cb:cb + 1], (seg, LANES))

    incl, strict, eye = m_ref[0], m_ref[1], m_ref[3]
    last = 0 if reverse else CH - 1

    def stack(x):
        return jnp.concatenate([x[:, h * LANES:(h + 1) * LANES] for h in range(n_heads)], axis=0)

    def body(t, carry):
        order = [t * group + j for j in range(group)]
        rss = [pl.ds(pl.multiple_of(((nch - 1 - c) if reverse else c) * CH, CH), CH) for c in order]
        gs = [stack(g_scr[rs, :]) for rs in rss]
        bes = [stack(be_scr[rs, :]) for rs in rss]
        gls = [jnp.concatenate([jnp.broadcast_to(g[h * CH + last:h * CH + last + 1], (CH, LANES))
                                for h in range(n_heads)], axis=0) for g in gs]
        egs = [jnp.exp(g) for g in gs]
        g2s = [jnp.concatenate([g] * (n // LANES), axis=1) if n >= LANES else g[:, :n] for g in gs]
        grows = [jnp.sum(g2 * eye, axis=0, keepdims=True) for g2 in g2s]
        dms = [jnp.exp(jnp.minimum(g2 - grow, 0.0)) * incl for g2, grow in zip(g2s, grows)]
        qss = [stack(q_ref[rs, :]) for rs in rss]
        kss = [stack(k_ref[rs, :]) for rs in rss]
        ksfs = [ks.astype(F32) for ks in kss]
        kbs = [ksf * be for ksf, be in zip(ksfs, bes)]
        grams = [_dot_nt(jnp.concatenate([kb.astype(BF16), qs], axis=0), ks) for kb, qs, ks in zip(kbs, qss, kss)]
        lms = [gram[:n] * dm * strict for gram, dm in zip(grams, dms)]
        attns = [(gram[n:] * dm).astype(BF16) for gram, dm in zip(grams, dms)]
        xs = [eye - lm * m_ref[4] for lm in lms]
        lvl = 5
        s = 2
        while s < CH:
            mask = m_ref[lvl]
            xbs = [x.astype(BF16) for x in xs]
            offs = [(lm * mask).astype(BF16) for lm in lms]
            tmp = [_dot(xb, off).astype(BF16) for xb, off in zip(xbs, offs)]
            xs = [x - _dot(t1, xb) for x, t1, xb in zip(xs, tmp, xbs)]
            lvl += 1
            s *= 2
        rhss = [jnp.concatenate([stack(v_ref[rs, :]).astype(F32) * be, kb * eg], axis=1)
                for rs, be, kb, eg in zip(rss, bes, kbs, egs)]
        sols = [rhs + _dot((x - eye).astype(BF16), rhs.astype(BF16)) for x, rhs in zip(xs, rhss)]
        qds = [(qs.astype(F32) * eg).astype(BF16) for qs, eg in zip(qss, egs)]
        kds = [(ksf * jnp.exp(gl - g)).astype(BF16) for ksf, gl, g in zip(ksfs, gls, gs)]
        for j in range(group):
            rs, gl, sol, qd, kd, attn = rss[j], gls[j], sols[j], qds[j], kds[j], attns[j]
            u, wb = sol[:, :HD], sol[:, HD:].astype(BF16)
            ws_parts, qs_parts = [], []
            for h in range(n_heads):
                sh = s_scr[h].astype(BF16)
                hs = slice(h * CH, (h + 1) * CH)
                both = _dot(jnp.concatenate([wb[hs], qd[hs]], axis=0), sh)
                ws_parts.append(both[:CH])
                qs_parts.append(both[CH:])
            vnew = u - jnp.concatenate(ws_parts, axis=0)
            vnb = vnew.astype(BF16)
            o = jnp.concatenate(qs_parts, axis=0) + _dot(attn, vnb)
            for h in range(n_heads):
                hs = slice(h * CH, (h + 1) * CH)
                decay = jnp.exp(gl[h * CH:h * CH + 1, :])
                s_scr[h] = s_scr[h] * decay + _dot_tn(kd[hs], vnb[hs])
                o_ref[rs, h * HD:(h + 1) * HD] = o[hs]
        return carry

    lax.fori_loop(0, nch // group, body, 0)

    @pl.when(sidx == pl.num_programs(1) - 1)
    def _():
        sfin_ref[...] = s_scr[...]


def _gdn_scan(qkv, gates, alog_row, dt_row, s0, dims, ts, reverse, d_idx):
    rows = qkv.shape[0]
    nh = dims.gdn_heads
    w = nh * HD
    seg = min(1024, ts)
    nseg = ts // seg
    masks = jnp.asarray(_gdn_masks(nh, reverse))

    def rb(b, s):
        return b * nseg + ((nseg - 1 - s) if reverse else s)

    return pl.pallas_call(
        functools.partial(_gdn_scan_kernel, n_heads=nh, reverse=reverse, d_idx=d_idx, group=math.gcd(4, seg // CH)),
        grid=(rows // ts, nseg),
        in_specs=[
            pl.BlockSpec((seg, w), lambda b, s: (rb(b, s), 0)),
            pl.BlockSpec((seg, w), lambda b, s: (rb(b, s), 1)),
            pl.BlockSpec((seg, w), lambda b, s: (rb(b, s), 2)),
            pl.BlockSpec((seg, LANES), lambda b, s: (rb(b, s), 0)),
            pl.BlockSpec((1, LANES), lambda b, s: (0, 0)),
            pl.BlockSpec((1, LANES), lambda b, s: (0, 0)),
            pl.BlockSpec(masks.shape, lambda b, s: (0, 0, 0)),
            pl.BlockSpec((None, nh, HD, HD), lambda b, s: (b, 0, 0, 0)),
        ],
        out_specs=[
            pl.BlockSpec((seg, w), lambda b, s: (rb(b, s), 0)),
            pl.BlockSpec((None, nh, HD, HD), lambda b, s: (b, 0, 0, 0)),
        ],
        out_shape=[
            jax.ShapeDtypeStruct((rows, w), F32),
            jax.ShapeDtypeStruct(s0.shape, F32),
        ],
        scratch_shapes=[
            pltpu.VMEM((nh, HD, HD), F32),
            pltpu.VMEM((seg, w), F32),
            pltpu.VMEM((seg, w), F32),
        ],
        compiler_params=_cparams(("parallel", "arbitrary")),
        name="gdn_scan_rev" if reverse else "gdn_scan_fwd",
    )(qkv, qkv, qkv, gates, alog_row, dt_row, masks, s0)


def _gdn_out_kernel(of_ref, ob_ref, z_ref, g_ref, y_ref, *, n_heads):
    g = g_ref[...]
    for h in range(n_heads):
        cs = slice(h * HD, (h + 1) * HD)
        o = of_ref[:, cs] + ob_ref[:, cs]
        y = o * lax.rsqrt(jnp.mean(o * o, axis=-1, keepdims=True) + EPS) * g
        y_ref[:, cs] = (y * _silu(z_ref[:, cs].astype(F32))).astype(y_ref.dtype)


def _gdn_out(o_f, o_b, p, norm_g, dims, tr=512):
    rows, w = o_f.shape
    tr = min(tr, rows)
    zb = dims.off_z // w
    return pl.pallas_call(
        functools.partial(_gdn_out_kernel, n_heads=dims.gdn_heads),
        grid=(rows // tr,),
        in_specs=[
            pl.BlockSpec((tr, w), lambda i: (i, 0)),
            pl.BlockSpec((tr, w), lambda i: (i, 0)),
            pl.BlockSpec((tr, w), lambda i: (i, zb)),
            pl.BlockSpec((1, HD), lambda i: (0, 0)),
        ],
        out_specs=pl.BlockSpec((tr, w), lambda i: (i, 0)),
        out_shape=jax.ShapeDtypeStruct((rows, w), BF16),
        compiler_params=_cparams(("parallel",)),
        name="gdn_out",
    )(o_f, o_b, p, norm_g.reshape(1, HD))


def _qk_prep_kernel(q_ref, k_ref, qg_ref, kg_ref, cos_ref, sin_ref, qo_ref, ko_ref, *, rope):
    if rope:
        cos, sin = cos_ref[...], sin_ref[...]

    def one(x, g, scale):
        x = x.astype(F32)
        y = x * lax.rsqrt(jnp.mean(x * x, axis=-1, keepdims=True) + EPS) * g
        if rope:
            y = y * cos + pltpu.roll(y, HD // 2, 1) * sin
        return y * scale

    for h in range(q_ref.shape[1] // HD):
        cs = slice(h * HD, (h + 1) * HD)
        qo_ref[:, cs] = one(q_ref[:, cs], qg_ref[...], Q_SCALE).astype(qo_ref.dtype)
    for h in range(k_ref.shape[1] // HD):
        cs = slice(h * HD, (h + 1) * HD)
        ko_ref[:, cs] = one(k_ref[:, cs], kg_ref[...], 1.0).astype(ko_ref.dtype)


def _qk_prep(p, q_g, k_g, cos, sin, dims, rope, tr=512):
    rows = p.shape[0]
    wq, wk = dims.w_gqa, dims.kv_heads * HD
    tr = min(tr, rows)
    nt = cos.shape[0] // tr if rope else 1
    tab = pl.BlockSpec((tr, HD), (lambda i: (i % nt, 0)) if rope else (lambda i: (0, 0)))
    return pl.pallas_call(
        functools.partial(_qk_prep_kernel, rope=rope),
        grid=(rows // tr,),
        in_specs=[
            pl.BlockSpec((tr, wq), lambda i: (i, dims.off_q // wq)),
            pl.BlockSpec((tr, wk), lambda i: (i, dims.off_k // wk)),
            pl.BlockSpec((1, HD), lambda i: (0, 0)),
            pl.BlockSpec((1, HD), lambda i: (0, 0)),
            tab, tab,
        ],
        out_specs=[
            pl.BlockSpec((tr, wq), lambda i: (i, 0)),
            pl.BlockSpec((tr, wk), lambda i: (i, 0)),
        ],
        out_shape=[
            jax.ShapeDtypeStruct((rows, wq), BF16),
            jax.ShapeDtypeStruct((rows, wk), BF16),
        ],
        compiler_params=_cparams(("parallel",)),
        name="qk_prep",
    )(p, p, q_g.reshape(1, HD), k_g.reshape(1, HD), cos, sin)


def _attn_kernel(*refs, n_kv, tk, hpg, n_part):
    q_ref, kv, o_ref = refs[0], refs[1:1 + 2 * n_kv], refs[1 + 2 * n_kv]
    tq = q_ref.shape[0]
    hpp = hpg // n_part
    qs = [jnp.concatenate([q_ref[:, h * HD:(h + 1) * HD] for h in range(pi * hpp, (pi + 1) * hpp)], axis=0)
          for pi in range(n_part)]
    m_rows = hpp * tq

    segs = []
    for sidx in range(n_kv):
        k_ref, v_ref = kv[2 * sidx], kv[2 * sidx + 1]
        t = min(tk, k_ref.shape[0])
        segs.append((k_ref, v_ref, t, k_ref.shape[0] // t))

    def tiles(seg, it):
        k_ref, v_ref, t, _ = seg
        rs = pl.ds(pl.multiple_of(it * t, t), t)
        return k_ref[rs, :], jnp.concatenate([v_ref[rs, :], jnp.ones((t, HD), BF16)], axis=1)

    def row_max(s):
        return jnp.max(s, axis=-1, keepdims=True)

    def sweep(step, carry, first_done):
        for si, seg in enumerate(segs):
            lo = 1 if (si == 0 and first_done) else 0
            if seg[3] - lo == 1:
                carry = step(*tiles(seg, lo), carry)
            elif seg[3] - lo > 1:
                carry = lax.fori_loop(lo, seg[3], lambda it, c, seg=seg: step(*tiles(seg, it), c), carry)
        return carry

    def write(accs):
        for pi, acc in enumerate(accs):
            out = acc[:, :HD] / acc[:, HD:]
            for hh in range(hpp):
                h = pi * hpp + hh
                o_ref[:, h * HD:(h + 1) * HD] = out[hh * tq:(hh + 1) * tq].astype(o_ref.dtype)

    k0, v0 = tiles(segs[0], 0)
    ms, accs = [], []
    for q in qs:
        s = _dot_nt(q, k0)
        ms.append(row_max(s))
        accs.append(_dot(jnp.exp2(s - ms[-1]).astype(BF16), v0))

    def fast_step(k, v1, accs):
        ss = [_dot_nt(q, k) for q in qs]
        return [acc + _dot(jnp.exp2(s - m).astype(BF16), v1) for s, m, acc in zip(ss, ms, accs)]

    accs = sweep(fast_step, accs, True)
    write(accs)
    overflowed = jnp.zeros((), F32)
    for acc in accs:
        overflowed = jnp.maximum(overflowed, jnp.max(jnp.where(jnp.isfinite(acc), 0.0, 1.0)))

    @pl.when(overflowed > 0.0)
    def _():
        def exact_step(k, v1, carry):
            out = []
            for q, (m, acc) in zip(qs, carry):
                s = _dot_nt(q, k)
                m_new = jnp.maximum(m, row_max(s))
                out.append((m_new, jnp.exp2(m - m_new) * acc + _dot(jnp.exp2(s - m_new).astype(BF16), v1)))
            return out

        init = [(jnp.full((m_rows, 1), -jnp.inf, F32), jnp.zeros((m_rows, 2 * HD), F32)) for _ in qs]
        write([acc for _, acc in sweep(exact_step, init, False)])


def _attention(q, kvs, dims, tq_rows, tq=512, tk=2048, n_part=2):
    rows = q.shape[0]
    hpg = dims.q_heads // dims.kv_heads
    gw = hpg * HD
    tq = min(tq, tq_rows)
    nqb = tq_rows // tq
    in_specs = [pl.BlockSpec((tq, gw), lambda b, g, i: (b * nqb + i, g))]
    args = [q]
    for k, v_src, v_blk, s_len in kvs:
        in_specs.append(pl.BlockSpec((s_len, HD), lambda b, g, i: (b, g)))
        in_specs.append(pl.BlockSpec((s_len, HD), lambda b, g, i, v_blk=v_blk: (b, v_blk + g)))
        args += [k, v_src]
    return pl.pallas_call(
        functools.partial(_attn_kernel, n_kv=len(kvs), tk=tk, hpg=hpg, n_part=math.gcd(n_part, hpg)),
        grid=(rows // tq_rows, dims.kv_heads, nqb),
        in_specs=in_specs,
        out_specs=pl.BlockSpec((tq, gw), lambda b, g, i: (b * nqb + i, g)),
        out_shape=jax.ShapeDtypeStruct((rows, dims.w_gqa), BF16),
        compiler_params=_cparams(("parallel", "parallel", "arbitrary")),
        name="attention",
    )(*args)


N_SEG = 8


def _lru_kernel(x_ref, gb_ref, cw_ref, cb_ref, wg_ref, bg_ref, lam_ref, h0_ref,
                y_ref, hlast_ref, xnat, xi, a_f, u_f, a_b, u_b, *, rc):
    ts = x_ref.shape[0]
    sl = ts // N_SEG
    cw, cb = cw_ref[...], cb_ref[...]
    wg, bg = wg_ref[...], bg_ref[...]
    sp = jax.nn.softplus(-lam_ref[...])

    def copy_body(r, carry):
        rows = pl.ds(pl.multiple_of(r * rc, rc), rc)
        xnat[rows, :] = x_ref[rows, :].astype(F32)
        return carry

    lax.fori_loop(0, ts // rc, copy_body, 0)

    def interleave_body(t, carry):
        xi[pl.ds(pl.multiple_of((t + 2) * N_SEG, N_SEG), N_SEG), :] = xnat[pl.ds(t, N_SEG, stride=sl), :]
        return carry

    lax.fori_loop(0, sl, interleave_body, 0, unroll=8)
    sub = lax.broadcasted_iota(jnp.int32, (N_SEG, LANES), 0)
    for t_dst, t_src in ((-2, sl - 2), (-1, sl - 1)):
        v = xi[(t_src + 2) * N_SEG:(t_src + 3) * N_SEG, :]
        xi[(t_dst + 2) * N_SEG:(t_dst + 3) * N_SEG, :] = jnp.where(sub == 0, 0.0, pltpu.roll(v, 1, 0))
    v = xi[2 * N_SEG:3 * N_SEG, :]
    xi[(sl + 2) * N_SEG:(sl + 3) * N_SEG, :] = jnp.where(sub == N_SEG - 1, 0.0, pltpu.roll(v, N_SEG - 1, 0))

    def gates_body(r, carry):
        start = pl.multiple_of(r * rc, rc)
        rows = pl.ds(start, rc)
        win = xi[pl.ds(start, rc + (LRU_CONV - 1) * N_SEG), :]
        xc = cb + sum(win[j * N_SEG:j * N_SEG + rc] * cw[j:j + 1] for j in range(LRU_CONV))
        gts = _sigmoid(_dot(xc.astype(BF16), wg) + bg)
        for d, (a_ref, u_ref) in enumerate(((a_f, u_f), (a_b, u_b))):
            rg = gts[:, (2 * d) * LANES:(2 * d + 1) * LANES]
            ig = gts[:, (2 * d + 1) * LANES:(2 * d + 2) * LANES]
            log_a = -LRU_C * rg * sp[d:d + 1]
            a = jnp.exp(log_a)
            scale = (1.0 + a) * jnp.sqrt(-jnp.tanh(0.5 * log_a))
            a_ref[rows, :] = a
            u_ref[rows, :] = scale * (ig * xc)
        return carry

    lax.fori_loop(0, ts // rc, gates_body, 0)

    zeros = jnp.zeros((N_SEG, LANES), F32)
    ones = jnp.ones((N_SEG, LANES), F32)

    def scan_step(a_ref, u_ref, tt, h, acc):
        rows = pl.ds(pl.multiple_of(tt * N_SEG, N_SEG), N_SEG)
        a = a_ref[rows, :]
        h = a * h + u_ref[rows, :]
        acc = a * acc
        u_ref[rows, :] = h
        a_ref[rows, :] = acc
        return h, acc

    def scan_body(t, carry):
        hf, af, hb, ab = carry
        hf, af = scan_step(a_f, u_f, t, hf, af)
        hb, ab = scan_step(a_b, u_b, sl - 1 - t, hb, ab)
        return hf, af, hb, ab

    hf, af, hb, ab = lax.fori_loop(0, sl, scan_body, (zeros, ones, zeros, ones), unroll=8)

    h0 = h0_ref[...]
    carry_f, carry_b = [h0[0:1]], [h0[1:2]]
    for s in range(N_SEG):
        carry_f.append(hf[s:s + 1] + af[s:s + 1] * carry_f[-1])
        sb = N_SEG - 1 - s
        carry_b.append(hb[sb:sb + 1] + ab[sb:sb + 1] * carry_b[-1])
    hlast_ref[...] = jnp.concatenate([carry_f[-1], carry_b[-1], jnp.zeros((6, LANES), F32)], axis=0)
    enter_f = jnp.concatenate(carry_f[:N_SEG], axis=0)
    enter_b = jnp.concatenate(carry_b[:N_SEG][::-1], axis=0)

    def stitch_body(t, carry):
        rows = pl.ds(pl.multiple_of(t * N_SEG, N_SEG), N_SEG)
        h = (u_f[rows, :] + a_f[rows, :] * enter_f) + (u_b[rows, :] + a_b[rows, :] * enter_b)
        xnat[pl.ds(t, N_SEG, stride=sl), :] = h
        return carry

    lax.fori_loop(0, sl, stitch_body, 0, unroll=8)

    def out_body(r, carry):
        rows = pl.ds(pl.multiple_of(r * rc, rc), rc)
        y_ref[rows, :] = (jax.nn.gelu(gb_ref[rows, :].astype(F32)) * xnat[rows, :]).astype(y_ref.dtype)
        return carry

    lax.fori_loop(0, ts // rc, out_body, 0)


def _lru(p, conv_w, conv_b, w_gates, b_gates, lam, h0, dims, ts):
    rows = p.shape[0]
    w = dims.w_lru
    nct = w // LANES
    xb, gb = dims.off_lx // LANES, dims.off_lg // LANES
    nb = rows // ts
    return pl.pallas_call(
        functools.partial(_lru_kernel, rc=min(512, ts)),
        grid=(nb, nct),
        in_specs=[
            pl.BlockSpec((ts, LANES), lambda b, c: (b, xb + c)),
            pl.BlockSpec((ts, LANES), lambda b, c: (b, gb + c)),
            pl.BlockSpec((LRU_CONV, LANES), lambda b, c: (0, c)),
            pl.BlockSpec((1, LANES), lambda b, c: (0, c)),
            pl.BlockSpec((None, LANES, 4 * LANES), lambda b, c: (c, 0, 0)),
            pl.BlockSpec((None, 1, 4 * LANES), lambda b, c: (c, 0, 0)),
            pl.BlockSpec((2, LANES), lambda b, c: (0, c)),
            pl.BlockSpec((None, 8, LANES), lambda b, c: (b, 0, c)),
        ],
        out_specs=[
            pl.BlockSpec((ts, LANES), lambda b, c: (b, c)),
            pl.BlockSpec((None, 8, LANES), lambda b, c: (b, 0, c)),
        ],
        out_shape=[
            jax.ShapeDtypeStruct((rows, w), BF16),
            jax.ShapeDtypeStruct((nb, 8, w), F32),
        ],
        scratch_shapes=[pltpu.VMEM((ts, LANES), F32), pltpu.VMEM((ts + (LRU_CONV - 1) * N_SEG, LANES), F32)]
        + [pltpu.VMEM((ts, LANES), F32)] * 4,
        compiler_params=_cparams(("parallel", "parallel")),
        name="lru",
    )(p, p, conv_w, conv_b, w_gates, b_gates, lam, h0)


def _final_norm_kernel(x_ref, g_ref, o_ref):
    x = x_ref[...]
    o_ref[...] = x * lax.rsqrt(jnp.mean(x * x, axis=-1, keepdims=True) + EPS) * g_ref[...]


def _final_norm(x, g, tr=512):
    rows, d = x.shape
    tr = min(tr, rows)
    return pl.pallas_call(
        _final_norm_kernel,
        grid=(rows // tr,),
        in_specs=[pl.BlockSpec((tr, d), lambda i: (i, 0)), pl.BlockSpec((1, d), lambda i: (0, 0))],
        out_specs=pl.BlockSpec((tr, d), lambda i: (i, 0)),
        out_shape=jax.ShapeDtypeStruct((rows, d), F32),
        compiler_params=_cparams(("parallel",)),
        name="final_norm",
    )(x, g.reshape(1, d))


def _rope_tables(dims):
    t = dims.seq
    rows = t // dims.grid_w
    row = jnp.repeat(jnp.arange(rows, dtype=F32), dims.grid_w)
    col = jnp.tile(jnp.arange(dims.grid_w, dtype=F32), rows)
    axis_dim = HD // 2
    inv_freq = ROPE_THETA ** (-jnp.arange(0, axis_dim, 2, dtype=F32) / axis_dim)
    ar, ac = row[:, None] * inv_freq, col[:, None] * inv_freq
    cos = jnp.concatenate([jnp.cos(ar), jnp.cos(ac)] * 2, axis=-1)
    sin = jnp.concatenate([-jnp.sin(ar), -jnp.sin(ac), jnp.sin(ar), jnp.sin(ac)], axis=-1)
    return cos, sin


def _pair_major(v):
    shape = v.shape
    v = v.reshape(shape[:-1] + (shape[-1] // HD, 2, 2, HD // 4))
    return jnp.swapaxes(v, -3, -2).reshape(shape)


def _split_w_in(w_in, dims):
    nh = dims.gdn_heads
    kv = dims.kv_heads * HD
    sizes = (3 * dims.w_gdn, dims.w_gdn, nh, nh, nh, nh, dims.w_gqa, kv, kv, dims.w_lru, dims.w_lru)
    bounds = np.cumsum(sizes)[:-1].tolist()
    qkv, z, b_f, b_b, a_f, a_b, gq, gk, gv, lx, lg = jnp.split(w_in, bounds, axis=-1)
    main = jnp.concatenate([qkv, z, _pair_major(gq), _pair_major(gk), gv, lx, lg], axis=-1).astype(BF16)
    pad = jnp.zeros(w_in.shape[:2] + (LANES - 4 * nh,), w_in.dtype)
    gate = jnp.concatenate([b_f, b_b, a_f, a_b, pad], axis=-1).astype(BF16)
    return main, gate


def _gate_rows(v, dims):
    nh = dims.gdn_heads
    flat = v.reshape(v.shape[0], 1, 2 * nh).astype(F32)
    return jnp.pad(flat, ((0, 0), (0, 0), (2 * nh, LANES - 4 * nh)))


def _lru_gate_weights(gate_w, gate_b, dims):
    depth = gate_w.shape[0]
    bs = dims.w_lru // LRU_BLOCKS
    per = LANES // bs
    nct = dims.w_lru // LANES
    w = gate_w.reshape(depth, 4, nct, per, bs, bs)
    eye = jnp.eye(per, dtype=gate_w.dtype)
    dense = jnp.einsum("lgcpde,pq->lcpdgqe", w, eye).reshape(depth, nct, LANES, 4 * LANES)
    bias = gate_b.reshape(depth, 4, nct, LANES).transpose(0, 2, 1, 3).reshape(depth, nct, 1, 4 * LANES)
    return dense.astype(BF16), bias.astype(F32)


def _forward(dims, x, c, ctx, c_ctx, ada_w, ada_b, norm1_g, norm2_g, w_in, gdn_conv_w, gdn_a_log, gdn_dt_bias,
             gdn_norm_g, q_norm_g, k_norm_g, lru_conv_w, lru_conv_b, lru_gate_w, lru_gate_b, lru_lambda,
             w_out, ffn_w_up, ffn_conv_w, ffn_conv_b, ffn_w_down, final_norm_g):
    d, b, t, ct, depth = dims.d, dims.batch, dims.seq, dims.ctx, dims.depth
    f, fp = dims.ffn, dims.ffn_pad
    nh = dims.gdn_heads
    tm = min(1024, t)

    xl = x.reshape(b * t, d)
    xc = ctx.reshape(b * ct, d)

    cond8 = jnp.concatenate([c, c_ctx[None, :], jnp.zeros((8 - b - 1, d), F32)], axis=0)
    mods = _adaln(cond8, ada_w, ada_b).reshape(depth, 8, N_MOD, 1, d)
    blocks_per_seq = t // tm
    mrow_l = lambda i: i // blocks_per_seq
    mrow_c = lambda i: b

    w_main, w_gate = _split_w_in(w_in, dims)
    alog_rows, dt_rows = _gate_rows(gdn_a_log, dims), _gate_rows(gdn_dt_bias, dims)
    lru_w, lru_b = _lru_gate_weights(lru_gate_w, lru_gate_b, dims)
    wo = w_out.astype(BF16)
    wo_parts = (wo[:, :dims.w_gdn], wo[:, dims.w_gdn:dims.w_gdn + dims.w_gqa], wo[:, dims.w_gdn + dims.w_gqa:])
    padc = ((0, 0), (0, 0), (0, fp - f))
    wu_g = jnp.pad(ffn_w_up[:, :, :f], padc).astype(BF16)
    wu_u = jnp.pad(ffn_w_up[:, :, f:], padc).astype(BF16)
    cw_g, cw_u = jnp.pad(ffn_conv_w[:, :, :f], padc), jnp.pad(ffn_conv_w[:, :, f:], padc)
    cb = ffn_conv_b[:, None, :]
    cb_g, cb_u = jnp.pad(cb[:, :, :f], padc), jnp.pad(cb[:, :, f:], padc)
    wd = jnp.pad(ffn_w_down, ((0, 0), (0, fp - f), (0, 0))).astype(BF16)
    cos, sin = _rope_tables(dims)
    v_blk = dims.off_v // HD

    s_zero = jnp.zeros((b, nh, HD, HD), F32)
    h_zero = jnp.zeros((b, 8, dims.w_lru), F32)

    for l in range(depth):
        ctx_out = l < depth - 1
        p_l, g_l = _in_proj(xl, mods, l, mrow_l, norm1_g[l], w_main, w_gate, tm)
        p_c, g_c = _in_proj(xc, mods, l, mrow_c, norm1_g[l], w_main, w_gate, tm)

        qkv_l = _gdn_prep(p_l, gdn_conv_w[l], dims, t)
        qkv_c = _gdn_prep(p_c, gdn_conv_w[l], dims, ct)
        o_l, o_c = [], []
        for di, rev in enumerate((False, True)):
            oc, sc = _gdn_scan(qkv_c, g_c, alog_rows[l], dt_rows[l], s_zero, dims, ct, rev, di)
            ol, _ = _gdn_scan(qkv_l, g_l, alog_rows[l], dt_rows[l], sc, dims, t, rev, di)
            o_l.append(ol)
            o_c.append(oc)
        ya_l = _gdn_out(o_l[0], o_l[1], p_l, gdn_norm_g[l], dims)

        qg, kg = _pair_major(q_norm_g[l]), _pair_major(k_norm_g[l])
        q_l, k_l = _qk_prep(p_l, qg, kg, cos, sin, dims, True)
        q_c, k_c = _qk_prep(p_c, qg, kg, cos, sin, dims, False)
        yb_l = _attention(q_l, [(k_l, p_l, v_blk, t), (k_c, p_c, v_blk, ct)], dims, t)

        yc_c, h_c = _lru(p_c, lru_conv_w[l], lru_conv_b[l][None, :], lru_w[l], lru_b[l], lru_lambda[l], h_zero, dims, ct)
        yc_l, _ = _lru(p_l, lru_conv_w[l], lru_conv_b[l][None, :], lru_w[l], lru_b[l], lru_lambda[l], h_c, dims, t)

        ffn_args = (wu_g, wu_u, cw_g, cw_u, cb_g, cb_u)
        xl = _proj_res((ya_l, yb_l, yc_l), wo_parts, xl, mods, l, 2, mrow_l, tm, 1024)
        a_l = _ffn_up(xl, mods, l, mrow_l, norm2_g[l], *ffn_args, t, tm)
        xl = _proj_res(a_l, (wd,) * len(a_l), xl, mods, l, 5, mrow_l, tm, 512)
        if ctx_out:
            ya_c = _gdn_out(o_c[0], o_c[1], p_c, gdn_norm_g[l], dims)
            yb_c = _attention(q_c, [(k_c, p_c, v_blk, ct)], dims, ct)
            xc = _proj_res((ya_c, yb_c, yc_c), wo_parts, xc, mods, l, 2, mrow_c, tm, 1024)
            a_c = _ffn_up(xc, mods, l, mrow_c, norm2_g[l], *ffn_args, ct, tm)
            xc = _proj_res(a_c, (wd,) * len(a_c), xc, mods, l, 5, mrow_c, tm, 512)

    return _final_norm(xl, final_norm_g).reshape(b, t, d)


def kernel(x, c, ctx, c_ctx, ada_w, ada_b, norm1_g, norm2_g, w_in, gdn_conv_w, gdn_a_log, gdn_dt_bias, gdn_norm_g, q_norm_g, k_norm_g, lru_conv_w, lru_conv_b, lru_gate_w, lru_gate_b, lru_lambda, w_out, ffn_w_up, ffn_conv_w, ffn_conv_b, ffn_w_down, final_norm_g):
    b, t, d = x.shape
    dims = Dims(d=d, batch=b, seq=t, ctx=ctx.shape[1], depth=ada_w.shape[0], grid_w=64)
    return _forward(dims, x, c, ctx, c_ctx, ada_w, ada_b, norm1_g, norm2_g, w_in, gdn_conv_w, gdn_a_log,
                    gdn_dt_bias, gdn_norm_g, q_norm_g, k_norm_g, lru_conv_w, lru_conv_b, lru_gate_w,
                    lru_gate_b, lru_lambda, w_out, ffn_w_up, ffn_conv_w, ffn_conv_b, ffn_w_down, final_norm_g)
```
